```python
import math
import jax, jax.numpy as jnp
from jax import lax
import numpy as np

D_MODEL = 2048
BATCH = 4
SEQ = 2048
DEPTH = 4
DEC_BATCH = 128
DEC_SEQ = 8
PAST_LEN = 16384
PAGE_SIZE = 128

N_MIXERS = 2
N_RET_LAYERS = (DEPTH + 1) // 2
N_CONV_LAYERS = DEPTH // 2
N_META = 16
RET_HEADS = 8
RET_DK = D_MODEL // RET_HEADS
RET_DV = 2 * RET_DK
RET_QK = RET_HEADS * RET_DK
RET_V = RET_HEADS * RET_DV
CHUNK = 128
CONV_WIDTH = 3
CONV_E = D_MODEL
ROPE_BASE = 10000.0
LN_EPS = 1e-5
GN_EPS = 1e-6
DEEPNORM_ALPHA = (2.0 * DEPTH) ** 0.25
DEEPNORM_BETA = (8.0 * DEPTH) ** -0.25

kernel_name = "retnet_shortconv_hybrid_step"


def _decay_log():
    return jnp.log1p(-jnp.exp2(-5.0 - jnp.arange(RET_HEADS, dtype=jnp.float32)))


def layer_norm(x, g, b):
    xf = x.astype(jnp.float32)
    mu = jnp.mean(xf, axis=-1, keepdims=True)
    var = jnp.mean(jnp.square(xf - mu), axis=-1, keepdims=True)
    y = (xf - mu) * lax.rsqrt(var + LN_EPS)
    return (y * g.astype(jnp.float32) + b.astype(jnp.float32)).astype(x.dtype)


def rotary(t, pos):
    half = t.shape[-1] // 2
    freqs = ROPE_BASE ** (-jnp.arange(half, dtype=jnp.float32) * 2.0 / t.shape[-1])
    ang = pos.astype(jnp.float32)[:, None] * freqs[None, :]
    cos = jnp.cos(ang).astype(t.dtype)
    sin = jnp.sin(ang).astype(t.dtype)
    t1, t2 = t[..., :half], t[..., half:]
    return jnp.concatenate([t1 * cos - t2 * sin, t1 * sin + t2 * cos], axis=-1)


def retention_chunk(q, k, v, S):
    dt = q.dtype
    L = q.shape[2]
    lg = _decay_log()
    idx = jnp.arange(L, dtype=jnp.float32)
    diff = idx[:, None] - idx[None, :]
    dmask = jnp.exp(jnp.where(diff[None] >= 0, diff[None] * lg[:, None, None], -jnp.inf)).astype(dt)
    q_decay = jnp.exp((idx + 1.0)[None, :] * lg[:, None]).astype(dt)
    k_decay = jnp.exp((L - 1.0 - idx)[None, :] * lg[:, None]).astype(dt)
    s_decay = jnp.exp(L * lg).astype(dt)
    scores = jnp.einsum('bhld,bhmd->bhlm', q, k) * dmask
    out = (jnp.einsum('bhlm,bhmv->bhlv', scores, v)
           + jnp.einsum('bhld,bhdv->bhlv', q, S) * q_decay[:, :, None])
    S_new = S * s_decay[:, None, None] + jnp.einsum('bhmd,bhmv->bhdv', k * k_decay[:, :, None], v)
    return out, S_new


def retention_prompt(q, k, v):
    B = q.shape[0]
    S0 = jnp.zeros((B, RET_HEADS, RET_DK, RET_DV), q.dtype)
    out_meta, S = retention_chunk(q[:, :, :N_META], k[:, :, :N_META], v[:, :, :N_META], S0)

    def to_chunks(t):
        t = t[:, :, N_META:]
        b, h, l, d = t.shape
        return t.reshape(b, h, l // CHUNK, CHUNK, d).transpose(2, 0, 1, 3, 4)

    def step(S, qkv):
        qc, kc, vc = qkv
        o, S = retention_chunk(qc, kc, vc, S)
        return S, o

    S, outs = lax.scan(step, S, (to_chunks(q), to_chunks(k), to_chunks(v)))
    outs = outs.transpose(1, 2, 0, 3, 4).reshape(B, RET_HEADS, -1, RET_DV)
    return jnp.concatenate([out_meta, outs], axis=2), S


def retention_layer(h, pos, w_in, w_out, state):
    B, L, _ = h.shape
    proj = h @ w_in
    q, k, v, g = jnp.split(proj, [RET_QK, 2 * RET_QK, 2 * RET_QK + RET_V], axis=-1)
    q = q.reshape(B, L, RET_HEADS, RET_DK).transpose(0, 2, 1, 3)
    k = k.reshape(B, L, RET_HEADS, RET_DK).transpose(0, 2, 1, 3)
    v = v.reshape(B, L, RET_HEADS, RET_DV).transpose(0, 2, 1, 3)
    q = rotary(q, pos) * jnp.asarray(RET_DK ** -0.5, q.dtype)
    k = rotary(k, pos)
    if state is None:
        y, S_new = retention_prompt(q, k, v)
    else:
        y, S_new = retention_chunk(q, k, v, state.astype(q.dtype))
    yf = y.astype(jnp.float32)
    mu = jnp.mean(yf, axis=-1, keepdims=True)
    var = jnp.mean(jnp.square(yf - mu), axis=-1, keepdims=True)
    yn = ((yf - mu) * lax.rsqrt(var + GN_EPS)).astype(h.dtype)
    yn = yn.transpose(0, 2, 1, 3).reshape(B, L, RET_V)
    return (jax.nn.silu(g) * yn) @ w_out, S_new


def conv_layer(h, w_in, conv_w, w_out, state):
    L = h.shape[1]
    proj = h @ w_in
    b_gate, c_gate, v, z = jnp.split(proj, 4, axis=-1)
    u = c_gate * v
    if state is None:
        up = jnp.pad(u, ((0, 0), (CONV_WIDTH - 1, 0), (0, 0)))
    else:
        up = jnp.concatenate([state.astype(u.dtype), u], axis=1)
    conv = sum(conv_w[j] * up[:, j:j + L] for j in range(CONV_WIDTH))
    new_state = up[:, -(CONV_WIDTH - 1):]
    return (jax.nn.silu(z) * b_gate * conv) @ w_out, new_state


def setup_inputs(seed: int = 0) -> dict:
    key = jax.random.key(seed)
    ks = jax.random.split(key, 13)
    f32 = jnp.float32
    x_prompt = jax.random.normal(ks[0], (BATCH, SEQ, D_MODEL), f32)
    x_sample = jax.random.normal(ks[1], (DEC_BATCH, DEC_SEQ, D_MODEL), f32)
    gam = 1.0 - np.exp2(-5.0 - np.arange(RET_HEADS, dtype=np.float32))
    s_scale = jnp.asarray((1.0 / (1.0 - gam)) ** 0.5, f32)[None, None, :, None, None]
    state_ret = jax.random.normal(ks[2], (N_RET_LAYERS, DEC_BATCH, RET_HEADS, RET_DK, RET_DV), f32) * s_scale
    state_conv = jax.random.normal(ks[3], (N_CONV_LAYERS, DEC_BATCH, CONV_WIDTH - 1, CONV_E), f32)
    meta = jax.random.normal(ks[4], (N_META, D_MODEL), f32)
    w_ret_in = jax.random.normal(ks[5], (N_RET_LAYERS, D_MODEL, 2 * RET_QK + 2 * RET_V), f32) * D_MODEL ** -0.5
    w_ret_out = (jax.random.normal(ks[6], (N_RET_LAYERS, RET_V, D_MODEL), f32)
                 * (2.0 / (RET_V + D_MODEL)) ** 0.5 * DEEPNORM_BETA)
    w_conv_in = jax.random.normal(ks[7], (N_CONV_LAYERS, D_MODEL, 4 * CONV_E), f32) * D_MODEL ** -0.5
    conv_w = jax.random.normal(ks[8], (N_CONV_LAYERS, CONV_WIDTH, CONV_E), f32) * CONV_WIDTH ** -0.5
    w_conv_out = (jax.random.normal(ks[9], (N_CONV_LAYERS, CONV_E, D_MODEL), f32)
                  * (2.0 / (CONV_E + D_MODEL)) ** 0.5 * DEEPNORM_BETA)
    ln_g = 1.0 + 0.02 * jax.random.normal(ks[10], (DEPTH, D_MODEL), f32)
    ln_b = 0.02 * jax.random.normal(ks[11], (DEPTH, D_MODEL), f32)
    return {"x_prompt": x_prompt, "x_sample": x_sample, "state_ret": state_ret, "state_conv": state_conv,
            "meta": meta, "w_ret_in": w_ret_in, "w_ret_out": w_ret_out, "w_conv_in": w_conv_in,
            "conv_w": conv_w, "w_conv_out": w_conv_out, "ln_g": ln_g, "ln_b": ln_b}


def reference(x_prompt, x_sample, state_ret, state_conv, meta, w_ret_in, w_ret_out,
              w_conv_in, conv_w, w_conv_out, ln_g, ln_b):
    B = x_prompt.shape[0]
    xp = jnp.concatenate(
        [jnp.broadcast_to(meta[None].astype(x_prompt.dtype), (B, N_META, D_MODEL)), x_prompt], axis=1)
    xs = x_sample
    pos_p = jnp.arange(N_META + x_prompt.shape[1], dtype=jnp.int32)
    pos_s = PAST_LEN + jnp.arange(x_sample.shape[1], dtype=jnp.int32)
    alpha = jnp.asarray(DEEPNORM_ALPHA, xp.dtype)
    ret_p, ret_s, conv_p, conv_s = [], [], [], []
    for i in range(DEPTH):
        j = i // N_MIXERS
        if i % N_MIXERS == 0:
            mp, sp = retention_layer(xp, pos_p, w_ret_in[j], w_ret_out[j], None)
            ms, ss = retention_layer(xs, pos_s, w_ret_in[j], w_ret_out[j], state_ret[j])
            ret_p.append(sp)
            ret_s.append(ss)
        else:
            mp, sp = conv_layer(xp, w_conv_in[j], conv_w[j], w_conv_out[j], None)
            ms, ss = conv_layer(xs, w_conv_in[j], conv_w[j], w_conv_out[j], state_conv[j])
            conv_p.append(sp)
            conv_s.append(ss)
        xp = layer_norm(alpha * xp + mp, ln_g[i], ln_b[i])
        xs = layer_norm(alpha * xs + ms, ln_g[i], ln_b[i])
    y_prompt = xp[:, N_META:]
    y_sample = xs
    return (y_prompt, y_sample, jnp.stack(ret_p), jnp.stack(ret_s), jnp.stack(conv_p), jnp.stack(conv_s))
```

```python
import functools
import math

import jax
import jax.numpy as jnp
from jax import lax
from jax.experimental import pallas as pl
from jax.experimental.pallas import tpu as pltpu

D_MODEL = 2048
DEPTH = 4
PAST_LEN = 16384
N_META = 16
RET_HEADS = 8
RET_DK = D_MODEL // RET_HEADS
RET_DV = 2 * RET_DK
RET_QK = RET_HEADS * RET_DK
RET_V = RET_HEADS * RET_DV
CONV_WIDTH = 3
CONV_E = D_MODEL
ROPE_BASE = 10000.0
LN_EPS = 1e-5
GN_EPS = 1e-6
DEEPNORM_ALPHA = (2.0 * DEPTH) ** 0.25

HALF = RET_DK // 2
PROMPT_CHUNK = 256
SAMPLE_GROUP = 2
VMEM_LIMIT = 56 * 1024 * 1024

BF16 = jnp.bfloat16
F32 = jnp.float32


def _params(*sem):
    return pltpu.CompilerParams(dimension_semantics=sem, vmem_limit_bytes=VMEM_LIMIT)


def _dot(a, b):
    return jnp.dot(a, b, preferred_element_type=F32)


def _dot_nt(a, b):
    return lax.dot_general(a, b, (((1,), (1,)), ((), ())), preferred_element_type=F32)


def _dot_tn(a, b):
    return lax.dot_general(a, b, (((0,), (0,)), ((), ())), preferred_element_type=F32)


def _silu(x):
    return x * (1.0 / (1.0 + jnp.exp(-x)))


def _proj_kernel(x_ref, w_ref, o_ref):
    o_ref[...] = _dot(x_ref[...], w_ref[...]).astype(o_ref.dtype)


def _proj_rope_kernel(x_ref, w_ref, cos_ref, sin_ref, o_ref, *, tn, q_tiles):
    acc = _dot(x_ref[...], w_ref[...])
    c = cos_ref[...]
    s = sin_ref[...]
    scale = jnp.where(pl.program_id(1) < q_tiles, RET_DK ** -0.5, 1.0).astype(F32)
    for h in range(tn // RET_DK):
        t1 = acc[:, h * RET_DK:h * RET_DK + HALF]
        t2 = acc[:, h * RET_DK + HALF:(h + 1) * RET_DK]
        o_ref[:, h * RET_DK:h * RET_DK + HALF] = ((t1 * c - t2 * s) * scale).astype(o_ref.dtype)
        o_ref[:, h * RET_DK + HALF:(h + 1) * RET_DK] = ((t1 * s + t2 * c) * scale).astype(o_ref.dtype)


def _proj(x, w, col_off, n_cols, tm, tn, out_dtype, rope=None):
    m, k = x.shape
    grid = (m // tm, n_cols // tn)
    x_spec = pl.BlockSpec((tm, k), lambda i, j: (i, 0))
    w_spec = pl.BlockSpec((k, tn), lambda i, j: (0, col_off // tn + j))
    o_spec = pl.BlockSpec((tm, tn), lambda i, j: (i, j))
    out_shape = jax.ShapeDtypeStruct((m, n_cols), out_dtype)
    if rope is None:
        return pl.pallas_call(
            _proj_kernel, grid=grid, in_specs=[x_spec, w_spec], out_specs=o_spec,
            out_shape=out_shape, compiler_params=_params("parallel", "parallel"),
            name="proj")(x, w)
    cos, sin = rope
    t_spec = pl.BlockSpec((tm, HALF), lambda i, j: (i % (cos.shape[0] // tm), 0))
    kern = functools.partial(_proj_rope_kernel, tn=tn, q_tiles=RET_QK // tn)
    return pl.pallas_call(
        kern, grid=grid, in_specs=[x_spec, w_spec, t_spec, t_spec], out_specs=o_spec,
        out_shape=out_shape, compiler_params=_params("parallel", "parallel"),
        name="proj_rope")(x, w, cos, sin)


def _out_ln_kernel(a_ref, w_ref, x_ref, g_ref, b_ref, y_ref, ybf_ref, acc_ref, *, k_steps):
    kk = pl.program_id(1)

    @pl.when(kk == 0)
    def _():
        acc_ref[...] = jnp.zeros_like(acc_ref)

    acc_ref[...] += _dot(a_ref[...], w_ref[...])

    @pl.when(kk == k_steps - 1)
    def _():
        y = DEEPNORM_ALPHA * x_ref[...] + acc_ref[...]
        mu = jnp.mean(y, axis=-1, keepdims=True)
        d = y - mu
        var = jnp.mean(d * d, axis=-1, keepdims=True)
        out = d * lax.rsqrt(var + LN_EPS) * g_ref[...] + b_ref[...]
        y_ref[...] = out
        ybf_ref[...] = out.astype(BF16)


def _out_ln(a, w, x, g, b, tm, tk):
    m, k = a.shape
    n = w.shape[1]
    k_steps = k // tk
    kern = functools.partial(_out_ln_kernel, k_steps=k_steps)
    return pl.pallas_call(
        kern, grid=(m // tm, k_steps),
        in_specs=[pl.BlockSpec((tm, tk), lambda i, kk: (i, kk)),
                  pl.BlockSpec((tk, n), lambda i, kk: (kk, 0)),
                  pl.BlockSpec((tm, n), lambda i, kk: (i, 0)),
                  pl.BlockSpec((1, n), lambda i, kk: (0, 0)),
                  pl.BlockSpec((1, n), lambda i, kk: (0, 0))],
        out_specs=[pl.BlockSpec((tm, n), lambda i, kk: (i, 0)),
                   pl.BlockSpec((tm, n), lambda i, kk: (i, 0))],
        out_shape=[jax.ShapeDtypeStruct((m, n), F32), jax.ShapeDtypeStruct((m, n), BF16)],
        scratch_shapes=[pltpu.VMEM((tm, n), F32)],
        compiler_params=_params("parallel", "arbitrary"),
        name="out_ln")(a, w, x, g, b)


def _group_norm_gate(y, g):
    mu = jnp.mean(y, axis=-1, keepdims=True)
    d = y - mu
    var = jnp.mean(d * d, axis=-1, keepdims=True)
    return _silu(g) * (d * lax.rsqrt(var + GN_EPS))


def _ret_prompt_kernel(sdec_ref, q_ref, k_ref, v_ref, g_ref, dm_ref, qd_ref, kd_ref, kdm_ref,
                       o_ref, s_out_ref, s_ref, *, n_chunks):
    h = pl.program_id(1)
    c_len = PROMPT_CHUNK

    qm = q_ref[0:N_META, :]
    km = k_ref[0:N_META, :]
    vm = v_ref[0:N_META, :]
    p = (_dot_nt(qm, km) * dm_ref[0, 0:N_META, 0:N_META]).astype(BF16)
    o_ref[0:N_META, :] = _group_norm_gate(_dot(p, vm), g_ref[0:N_META, :]).astype(o_ref.dtype)
    kdk = (km.astype(F32) * kdm_ref[0]).astype(BF16)
    s_ref[...] = _dot_tn(kdk, vm)

    sd = sdec_ref[h]

    def chunk(c, carry):
        r0 = pl.multiple_of(N_META + c * c_len, 16)
        qc = q_ref[pl.ds(r0, c_len), :]
        kc = k_ref[pl.ds(r0, c_len), :]
        vc = v_ref[pl.ds(r0, c_len), :]
        s = s_ref[...]
        p = (_dot_nt(qc, kc) * dm_ref[0]).astype(BF16)
        y = _dot(p, vc) + _dot(qc, s.astype(BF16)) * qd_ref[0]
        o_ref[pl.ds(r0, c_len), :] = _group_norm_gate(y, g_ref[pl.ds(r0, c_len), :]).astype(o_ref.dtype)
        kdk = (kc.astype(F32) * kd_ref[0]).astype(BF16)
        s_ref[...] = s * sd + _dot_tn(kdk, vc)
        return carry

    lax.fori_loop(0, n_chunks, chunk, 0)
    s_out_ref[0, 0] = s_ref[...]


def _ret_prompt(qk, v, g, tabs, batch, seq):
    c_len = PROMPT_CHUNK
    n_chunks = (seq - N_META) // c_len
    kern = functools.partial(_ret_prompt_kernel, n_chunks=n_chunks)
    hd = RET_HEADS
    return pl.pallas_call(
        kern, grid=(batch, hd),
        in_specs=[pl.BlockSpec(memory_space=pltpu.SMEM),
                  pl.BlockSpec((seq, RET_DK), lambda b, h: (b, h)),
                  pl.BlockSpec((seq, RET_DK), lambda b, h: (b, hd + h)),
                  pl.BlockSpec((seq, RET_DV), lambda b, h: (b, h)),
                  pl.BlockSpec((seq, RET_DV), lambda b, h: (b, h)),
                  pl.BlockSpec((1, c_len, c_len), lambda b, h: (h, 0, 0)),
                  pl.BlockSpec((1, c_len, 1), lambda b, h: (h, 0, 0)),
                  pl.BlockSpec((1, c_len, 1), lambda b, h: (h, 0, 0)),
                  pl.BlockSpec((1, N_META, 1), lambda b, h: (h, 0, 0))],
        out_specs=[pl.BlockSpec((seq, RET_DV), lambda b, h: (b, h)),
                   pl.BlockSpec((1, 1, RET_DK, RET_DV), lambda b, h: (b, h, 0, 0))],
        out_shape=[jax.ShapeDtypeStruct((batch * seq, RET_V), BF16),
                   jax.ShapeDtypeStruct((batch, hd, RET_DK, RET_DV), F32)],
        scratch_shapes=[pltpu.VMEM((RET_DK, RET_DV), F32)],
        compiler_params=_params("parallel", "parallel"),
        name="ret_prompt")(tabs["sdec_p"], qk, qk, v, g, tabs["dm_p"], tabs["qd_p"], tabs["kd_p"],
                           tabs["kd_m"])


def _ret_sample_kernel(sdec_ref, qk_ref, v_ref, g_ref, s_ref, dm_ref, qd_ref, kd_ref, *rest, seq):
    o_ref, s_out_ref, y_ref = rest[-3:]
    for gi in range(SAMPLE_GROUP):
        rows = slice(gi * seq, (gi + 1) * seq)
        for h in range(RET_HEADS):
            q = qk_ref[rows, h * RET_DK:(h + 1) * RET_DK]
            k = qk_ref[rows, RET_QK + h * RET_DK:RET_QK + (h + 1) * RET_DK]
            vb = v_ref[rows, h * RET_DV:(h + 1) * RET_DV].astype(BF16)
            qb = q.astype(BF16)
            s = s_ref[gi, h]
            p = (_dot_nt(qb, k.astype(BF16)) * dm_ref[h]).astype(BF16)
            y_ref[rows, h * RET_DV:(h + 1) * RET_DV] = (
                _dot(p, vb) + _dot(qb, s.astype(BF16)) * qd_ref[h])
            kdk = (k * kd_ref[h]).astype(BF16)
            s_out_ref[gi, h] = s * sdec_ref[h] + _dot_tn(kdk, vb)
    for h in range(RET_HEADS):
        cols = slice(h * RET_DV, (h + 1) * RET_DV)
        o_ref[:, cols] = _group_norm_gate(y_ref[:, cols], g_ref[:, cols]).astype(o_ref.dtype)


def _ret_sample(qk, v, g, state_all, layer, new_state_all, tabs, batch, seq):
    gsz = SAMPLE_GROUP
    rows = gsz * seq
    hd = RET_HEADS
    kern = functools.partial(_ret_sample_kernel, seq=seq)
    state_spec = pl.BlockSpec((None, gsz, hd, RET_DK, RET_DV), lambda b: (layer, b, 0, 0, 0))
    in_specs = [pl.BlockSpec(memory_space=pltpu.SMEM),
                pl.BlockSpec((rows, 2 * RET_QK), lambda b: (b, 0)),
                pl.BlockSpec((rows, RET_V), lambda b: (b, 0)),
                pl.BlockSpec((rows, RET_V), lambda b: (b, 0)),
                state_spec,
                pl.BlockSpec((hd, seq, seq), lambda b: (0, 0, 0)),
                pl.BlockSpec((hd, seq, 1), lambda b: (0, 0, 0)),
                pl.BlockSpec((hd, seq, 1), lambda b: (0, 0, 0))]
    args = [tabs["sdec_s"], qk, v, g, state_all, tabs["dm_s"], tabs["qd_s"], tabs["kd_s"]]
    aliases = {}
    if new_state_all is not None:
        in_specs.append(pl.BlockSpec(memory_space=pl.ANY))
        args.append(new_state_all)
        aliases = {len(args) - 1: 1}
    return pl.pallas_call(
        kern, grid=(batch // gsz,),
        in_specs=in_specs,
        out_specs=[pl.BlockSpec((rows, RET_V), lambda b: (b, 0)), state_spec],
        out_shape=[jax.ShapeDtypeStruct((batch * seq, RET_V), BF16),
                   jax.ShapeDtypeStruct(state_all.shape, F32)],
        scratch_shapes=[pltpu.VMEM((rows, RET_V), F32)],
        input_output_aliases=aliases,
        compiler_params=_params("parallel"),
        name="ret_sample")(*args)


def _conv_gate(x, wb_ref, wc_ref, wv_ref, wz_ref, cw_ref, prev1, prev2, period):
    u = _dot(x, wc_ref[...]) * _dot(x, wv_ref[...])
    t = lax.broadcasted_iota(jnp.int32, u.shape, 0) % period
    r1 = pltpu.roll(u, 1, 0)
    r2 = pltpu.roll(u, 2, 0)
    u1 = jnp.where(t >= 1, r1, prev1)
    u2 = jnp.where(t >= 2, r2, jnp.where(t == 1, prev1, prev2))
    conv = cw_ref[0:1, :] * u2 + cw_ref[1:2, :] * u1 + cw_ref[2:3, :] * u
    gated = _silu(_dot(x, wz_ref[...])) * _dot(x, wb_ref[...]) * conv
    return u, gated


def _conv_prompt_kernel(x_ref, wb_ref, wc_ref, wv_ref, wz_ref, cw_ref, o_ref, st_ref, *, seq):
    zero = jnp.zeros((), F32)
    u, gated = _conv_gate(x_ref[...], wb_ref, wc_ref, wv_ref, wz_ref, cw_ref, zero, zero, seq)
    o_ref[...] = gated.astype(o_ref.dtype)
    st_ref[0] = u[seq - (CONV_WIDTH - 1):seq, :]


def _conv_sample_kernel(x_ref, wb_ref, wc_ref, wv_ref, wz_ref, cw_ref, st_in_ref, o_ref, st_ref, *, seq):
    st = st_in_ref[...]
    nb, _, tn = st.shape
    prev2 = jnp.broadcast_to(st[:, 0:1, :], (nb, seq, tn)).reshape(nb * seq, tn)
    prev1 = jnp.broadcast_to(st[:, 1:2, :], (nb, seq, tn)).reshape(nb * seq, tn)
    u, gated = _conv_gate(x_ref[...], wb_ref, wc_ref, wv_ref, wz_ref, cw_ref, prev1, prev2, seq)
    o_ref[...] = gated.astype(o_ref.dtype)
    st_ref[...] = u.reshape(nb, seq, tn)[:, seq - (CONV_WIDTH - 1):seq, :]


def _conv_mixer(x, w, cw, state, batch, seq, tn):
    m, k = x.shape
    e = CONV_E
    nt = e // tn
    w_specs = [pl.BlockSpec((k, tn), functools.partial(lambda s, j, q: (0, q * nt + j), q=q))
               for q in range(4)]
    cw_spec = pl.BlockSpec((CONV_WIDTH, tn), lambda s, j: (0, j))
    if state is None:
        kern = functools.partial(_conv_prompt_kernel, seq=seq)
        return pl.pallas_call(
            kern, grid=(batch, nt),
            in_specs=[pl.BlockSpec((seq, k), lambda s, j: (s, 0))] + w_specs + [cw_spec],
            out_specs=[pl.BlockSpec((seq, tn), lambda s, j: (s, j)),
                       pl.BlockSpec((1, CONV_WIDTH - 1, tn), lambda s, j: (s, 0, j))],
            out_shape=[jax.ShapeDtypeStruct((m, e), BF16),
                       jax.ShapeDtypeStruct((batch, CONV_WIDTH - 1, e), F32)],
            compiler_params=_params("parallel", "parallel"),
            name="conv_prompt")(x, w, w, w, w, cw)
    kern = functools.partial(_conv_sample_kernel, seq=seq)
    return pl.pallas_call(
        kern, grid=(1, nt),
        in_specs=[pl.BlockSpec((m, k), lambda s, j: (0, 0))] + w_specs + [
            cw_spec, pl.BlockSpec((batch, CONV_WIDTH - 1, tn), lambda s, j: (0, 0, j))],
        out_specs=[pl.BlockSpec((m, tn), lambda s, j: (0, j)),
                   pl.BlockSpec((batch, CONV_WIDTH - 1, tn), lambda s, j: (0, 0, j))],
        out_shape=[jax.ShapeDtypeStruct((m, e), BF16),
                   jax.ShapeDtypeStruct((batch, CONV_WIDTH - 1, e), F32)],
        compiler_params=_params("parallel", "parallel"),
        name="conv_sample")(x, w, w, w, w, cw, state)


def _rope_tables(pos):
    freqs = ROPE_BASE ** (-jnp.arange(HALF, dtype=F32) * 2.0 / RET_DK)
    ang = pos.astype(F32)[:, None] * freqs[None, :]
    return jnp.cos(ang), jnp.sin(ang)


def _decay_tables(length):
    lg = jnp.log1p(-jnp.exp2(-5.0 - jnp.arange(RET_HEADS, dtype=F32)))
    idx = jnp.arange(length, dtype=F32)
    diff = idx[:, None] - idx[None, :]
    dmask = jnp.exp(jnp.where(diff[None] >= 0, diff[None] * lg[:, None, None], -jnp.inf))
    q_decay = jnp.exp((idx + 1.0)[None, :] * lg[:, None])[:, :, None]
    k_decay = jnp.exp((length - 1.0 - idx)[None, :] * lg[:, None])[:, :, None]
    s_decay = jnp.exp(length * lg)
    return dmask, q_decay, k_decay, s_decay


def kernel(x_prompt, x_sample, state_ret, state_conv, meta, w_ret_in, w_ret_out, w_conv_in, conv_w,
           w_conv_out, ln_g, ln_b):
    bp, lp0, d = x_prompt.shape
    bs, ls, _ = x_sample.shape
    lp = N_META + lp0

    xp = jnp.concatenate([jnp.broadcast_to(meta[None].astype(x_prompt.dtype), (bp, N_META, d)), x_prompt],
                         axis=1).reshape(bp * lp, d)
    xs = x_sample.reshape(bs * ls, d)
    xp_bf, xs_bf = xp.astype(BF16), xs.astype(BF16)

    cos_p, sin_p = _rope_tables(jnp.arange(lp, dtype=jnp.int32))
    cos_s, sin_s = _rope_tables(PAST_LEN + jnp.arange(ls, dtype=jnp.int32))
    cos_s, sin_s = jnp.tile(cos_s, (bs, 1)), jnp.tile(sin_s, (bs, 1))

    dm_p, qd_p, kd_p, sdec_p = _decay_tables(PROMPT_CHUNK)
    _, _, kd_m, _ = _decay_tables(N_META)
    dm_s, qd_s, kd_s, sdec_s = _decay_tables(ls)
    tabs = dict(dm_p=dm_p, qd_p=qd_p, kd_p=kd_p, sdec_p=sdec_p, kd_m=kd_m,
                dm_s=dm_s, qd_s=qd_s, kd_s=kd_s, sdec_s=sdec_s)

    ret_p, conv_p, conv_s = [], [], []
    ret_s = None
    for i in range(DEPTH):
        j = i // 2
        g_i, b_i = ln_g[i][None, :], ln_b[i][None, :]
        if i % 2 == 0:
            w_in, w_out = w_ret_in[j].astype(BF16), w_ret_out[j].astype(BF16)
            qk = _proj(xp_bf, w_in, 0, 2 * RET_QK, lp, 1024, BF16, rope=(cos_p, sin_p))
            v = _proj(xp_bf, w_in, 2 * RET_QK, RET_V, lp, 1024, BF16)
            g = _proj(xp_bf, w_in, 2 * RET_QK + RET_V, RET_V, lp, 512, F32)
            mp, sp = _ret_prompt(qk, v, g, tabs, bp, lp)
            ret_p.append(sp)
            qk = _proj(xs_bf, w_in, 0, 2 * RET_QK, bs * ls, 1024, F32, rope=(cos_s, sin_s))
            v = _proj(xs_bf, w_in, 2 * RET_QK, RET_V, bs * ls, 1024, F32)
            g = _proj(xs_bf, w_in, 2 * RET_QK + RET_V, RET_V, bs * ls, 1024, F32)
            ms, ret_s = _ret_sample(qk, v, g, state_ret, j, ret_s, tabs, bs, ls)
        else:
            w_in, w_out = w_conv_in[j].astype(BF16), w_conv_out[j].astype(BF16)
            mp, sp = _conv_mixer(xp_bf, w_in, conv_w[j], None, bp, lp, 256)
            ms, ss = _conv_mixer(xs_bf, w_in, conv_w[j], state_conv[j], bs, ls, 256)
            conv_p.append(sp)
            conv_s.append(ss)
        xp, xp_bf = _out_ln(mp, w_out, xp, g_i, b_i, 688, 1024)
        xs, xs_bf = _out_ln(ms, w_out, xs, g_i, b_i, 512, 1024)

    y_prompt = xp.reshape(bp, lp, d)[:, N_META:]
    y_sample = xs.reshape(bs, ls, d)
    return (y_prompt, y_sample, jnp.stack(ret_p), ret_s, jnp.stack(conv_p), jnp.stack(conv_s))
```

```python
import functools

import jax
import jax.numpy as jnp
from jax import lax
from jax.experimental import pallas as pl
from jax.experimental.pallas import tpu as pltpu

D_MODEL = 2048
DEPTH = 4
PAST_LEN = 16384
N_META = 16
RET_HEADS = 8
RET_DK = D_MODEL // RET_HEADS
RET_DV = 2 * RET_DK
RET_QK = RET_HEADS * RET_DK
RET_V = RET_HEADS * RET_DV
CONV_WIDTH = 3
CONV_E = D_MODEL
ROPE_BASE = 10000.0
LN_EPS = 1e-5
GN_EPS = 1e-6
DEEPNORM_ALPHA = (2.0 * DEPTH) ** 0.25

HALF = RET_DK // 2
SUBLANES = 8
PROMPT_CHUNK = 256
SAMPLE_GROUP = 2
MAIN_TILE = 1024
PROJ_TN = 1024
CONV_TN = 256
OUT_TM_MAIN = 512
OUT_TK = 1024
VMEM_LIMIT = 56 * 1024 * 1024

BF16 = jnp.bfloat16
F32 = jnp.float32


def _params(*sem):
    return pltpu.CompilerParams(dimension_semantics=sem, vmem_limit_bytes=VMEM_LIMIT)


def _dot(a, b):
    return jnp.dot(a, b, preferred_element_type=F32)


def _dot_nt(a, b):
    return lax.dot_general(a, b, (((1,), (1,)), ((), ())), preferred_element_type=F32)


def _dot_tn(a, b):
    return lax.dot_general(a, b, (((0,), (0,)), ((), ())), preferred_element_type=F32)


def _silu(x):
    return x * (1.0 / (1.0 + jnp.exp(-x)))


def _proj_kernel(xm_ref, xs_ref, w_ref, cm_ref, sm_ref, cs_ref, ss_ref, om_ref, os_ref, wbf_ref, *,
                 n_main, tn):
    j = pl.program_id(0)
    i = pl.program_id(1)
    rope_tiles = 2 * RET_QK // tn
    q_tiles = RET_QK // tn

    @pl.when(i == 0)
    def _():
        wbf_ref[...] = w_ref[...].astype(BF16)

    def path(x_ref, c_ref, s_ref, o_ref):
        acc = _dot(x_ref[...], wbf_ref[...])

        @pl.when(j < rope_tiles)
        def _():
            c = c_ref[...]
            s = s_ref[...]
            scale = jnp.where(j < q_tiles, RET_DK ** -0.5, 1.0).astype(F32)
            for h in range(tn // RET_DK):
                t1 = acc[:, h * RET_DK:h * RET_DK + HALF]
                t2 = acc[:, h * RET_DK + HALF:(h + 1) * RET_DK]
                o_ref[:, h * RET_DK:h * RET_DK + HALF] = ((t1 * c - t2 * s) * scale).astype(o_ref.dtype)
                o_ref[:, h * RET_DK + HALF:(h + 1) * RET_DK] = ((t1 * s + t2 * c) * scale).astype(o_ref.dtype)

        @pl.when(j >= rope_tiles)
        def _():
            o_ref[...] = acc.astype(o_ref.dtype)

    @pl.when(i < n_main)
    def _():
        path(xm_ref, cm_ref, sm_ref, om_ref)

    @pl.when(i == n_main)
    def _():
        path(xs_ref, cs_ref, ss_ref, os_ref)


def _ret_proj(xm, xs, w_all, layer, rope_m, rope_s, seq):
    rm, k = xm.shape
    rs = xs.shape[0]
    n = w_all.shape[2]
    tm, tn = MAIN_TILE, PROJ_TN
    n_main = rm // tm
    tiles_per_seq = seq // tm
    last = n_main - 1
    kern = functools.partial(_proj_kernel, n_main=n_main, tn=tn)
    return pl.pallas_call(
        kern, grid=(n // tn, n_main + 1),
        in_specs=[pl.BlockSpec((tm, k), lambda j, i: (jnp.minimum(i, last), 0)),
                  pl.BlockSpec((rs, k), lambda j, i: (0, 0)),
                  pl.BlockSpec((None, k, tn), lambda j, i: (layer, 0, j)),
                  pl.BlockSpec((tm, HALF), lambda j, i: (jnp.minimum(i, last) % tiles_per_seq, 0)),
                  pl.BlockSpec((tm, HALF), lambda j, i: (jnp.minimum(i, last) % tiles_per_seq, 0)),
                  pl.BlockSpec((rs, HALF), lambda j, i: (0, 0)),
                  pl.BlockSpec((rs, HALF), lambda j, i: (0, 0))],
        out_specs=[pl.BlockSpec((tm, tn), lambda j, i: (jnp.minimum(i, last), j)),
                   pl.BlockSpec((rs, tn), lambda j, i: (0, j))],
        out_shape=[jax.ShapeDtypeStruct((rm, n), BF16), jax.ShapeDtypeStruct((rs, n), BF16)],
        scratch_shapes=[pltpu.VMEM((k, tn), BF16)],
        compiler_params=_params("parallel", "arbitrary"),
        name="ret_proj")(xm, xs, w_all, rope_m[0], rope_m[1], rope_s[0], rope_s[1])


def _out_ln_kernel(a_ref, w_ref, x_ref, g_ref, b_ref, y_ref, ybf_ref, acc_ref, *, k_steps):
    kk = pl.program_id(1)

    @pl.when(kk == 0)
    def _():
        acc_ref[...] = jnp.zeros_like(acc_ref)

    acc_ref[...] += _dot(a_ref[...], w_ref[...])

    @pl.when(kk == k_steps - 1)
    def _():
        y = DEEPNORM_ALPHA * x_ref[...] + acc_ref[...]
        mu = jnp.mean(y, axis=-1, keepdims=True)
        d = y - mu
        var = jnp.mean(d * d, axis=-1, keepdims=True)
        out = d * lax.rsqrt(var + LN_EPS) * g_ref[...] + b_ref[...]
        y_ref[...] = out
        ybf_ref[...] = out.astype(BF16)


def _out_ln(a, w_all, layer, x, g, b, tm):
    m, k = a.shape
    n = w_all.shape[2]
    tk = OUT_TK
    k_steps = k // tk
    kern = functools.partial(_out_ln_kernel, k_steps=k_steps)
    return pl.pallas_call(
        kern, grid=(m // tm, k_steps),
        in_specs=[pl.BlockSpec((tm, tk), lambda i, kk: (i, kk)),
                  pl.BlockSpec((None, tk, n), lambda i, kk: (layer, kk, 0)),
                  pl.BlockSpec((tm, n), lambda i, kk: (i, 0)),
                  pl.BlockSpec((1, n), lambda i, kk: (0, 0)),
                  pl.BlockSpec((1, n), lambda i, kk: (0, 0))],
        out_specs=[pl.BlockSpec((tm, n), lambda i, kk: (i, 0)),
                   pl.BlockSpec((tm, n), lambda i, kk: (i, 0))],
        out_shape=[jax.ShapeDtypeStruct((m, n), F32), jax.ShapeDtypeStruct((m, n), BF16)],
        scratch_shapes=[pltpu.VMEM((tm, n), F32)],
        compiler_params=_params("parallel", "arbitrary"),
        name="out_ln")(a, w_all, x, g, b)


def _group_norm_gate(y, g):
    mu = jnp.mean(y, axis=-1, keepdims=True)
    d = y - mu
    var = jnp.mean(d * d, axis=-1, keepdims=True)
    return _silu(g.astype(F32)) * (d * lax.rsqrt(var + GN_EPS))


def _ret_prompt_kernel(sdec_ref, q_ref, k_ref, v_ref, g_ref, qm_ref, km_ref, vm_ref, gm_ref,
                       dm_ref, qd_ref, kd_ref, kdm_ref, *rest, n_chunks, n_sample_rows):
    o_ref, os_ref, s_out_ref, s_ref = rest[-4:]
    h = pl.program_id(0)
    b = pl.program_id(1)
    c_len = PROMPT_CHUNK

    @pl.when(b == 0)
    def _():
        os_ref[0:n_sample_rows, :] = jnp.zeros((n_sample_rows, RET_DV), os_ref.dtype)

    qm = qm_ref[...]
    km = km_ref[...]
    vm = vm_ref[...]
    p = (_dot_nt(qm, km) * dm_ref[0, 0:N_META, 0:N_META]).astype(BF16)
    r_meta = pl.multiple_of(n_sample_rows + b * N_META, N_META)
    os_ref[pl.ds(r_meta, N_META), :] = _group_norm_gate(_dot(p, vm), gm_ref[...]).astype(os_ref.dtype)
    kdk = (km.astype(F32) * kdm_ref[0]).astype(BF16)
    s_ref[...] = _dot_tn(kdk, vm)

    sd = sdec_ref[h]

    def chunk(c, carry):
        r0 = pl.multiple_of(c * c_len, c_len)
        qc = q_ref[pl.ds(r0, c_len), :]
        kc = k_ref[pl.ds(r0, c_len), :]
        vc = v_ref[pl.ds(r0, c_len), :]
        s = s_ref[...]
        p = (_dot_nt(qc, kc) * dm_ref[0]).astype(BF16)
        y = _dot(p, vc) + _dot(qc, s.astype(BF16)) * qd_ref[0]
        o_ref[pl.ds(r0, c_len), :] = _group_norm_gate(y, g_ref[pl.ds(r0, c_len), :]).astype(o_ref.dtype)
        kdk = (kc.astype(F32) * kd_ref[0]).astype(BF16)
        s_ref[...] = s * sd + _dot_tn(kdk, vc)
        return carry

    lax.fori_loop(0, n_chunks, chunk, 0, unroll=2)
    s_out_ref[...] = s_ref[...]


def _ret_prompt(pm, ps, layer, new_state_all, n_layers, tabs, batch, seq, n_sample_rows):
    c_len = PROMPT_CHUNK
    hd = RET_HEADS
    rs = ps.shape[0]
    meta0 = n_sample_rows // N_META
    qk_blocks = RET_QK // RET_DK
    v_blocks = 2 * RET_QK // RET_DV
    g_blocks = v_blocks + RET_V // RET_DV
    kern = functools.partial(_ret_prompt_kernel, n_chunks=seq // c_len, n_sample_rows=n_sample_rows)
    state_spec = pl.BlockSpec((None, None, None, RET_DK, RET_DV), lambda h, b: (layer, b, h, 0, 0))
    in_specs = [pl.BlockSpec(memory_space=pltpu.SMEM),
                pl.BlockSpec((seq, RET_DK), lambda h, b: (b, h)),
                pl.BlockSpec((seq, RET_DK), lambda h, b: (b, qk_blocks + h)),
                pl.BlockSpec((seq, RET_DV), lambda h, b: (b, v_blocks + h)),
                pl.BlockSpec((seq, RET_DV), lambda h, b: (b, g_blocks + h)),
                pl.BlockSpec((N_META, RET_DK), lambda h, b: (meta0 + b, h)),
                pl.BlockSpec((N_META, RET_DK), lambda h, b: (meta0 + b, qk_blocks + h)),
                pl.BlockSpec((N_META, RET_DV), lambda h, b: (meta0 + b, v_blocks + h)),
                pl.BlockSpec((N_META, RET_DV), lambda h, b: (meta0 + b, g_blocks + h)),
                pl.BlockSpec((1, c_len, c_len), lambda h, b: (h, 0, 0)),
                pl.BlockSpec((1, c_len, 1), lambda h, b: (h, 0, 0)),
                pl.BlockSpec((1, c_len, 1), lambda h, b: (h, 0, 0)),
                pl.BlockSpec((1, N_META, 1), lambda h, b: (h, 0, 0))]
    args = [tabs["sdec_p"], pm, pm, pm, pm, ps, ps, ps, ps, tabs["dm_p"], tabs["qd_p"], tabs["kd_p"],
            tabs["kd_m"]]
    aliases = {}
    if new_state_all is not None:
        in_specs.append(pl.BlockSpec(memory_space=pl.ANY))
        args.append(new_state_all)
        aliases = {len(args) - 1: 2}
    return pl.pallas_call(
        kern, grid=(hd, batch),
        in_specs=in_specs,
        out_specs=[pl.BlockSpec((seq, RET_DV), lambda h, b: (b, h)),
                   pl.BlockSpec((rs, RET_DV), lambda h, b: (0, h)),
                   state_spec],
        out_shape=[jax.ShapeDtypeStruct((batch * seq, RET_V), BF16),
                   jax.ShapeDtypeStruct((rs, RET_V), BF16),
                   jax.ShapeDtypeStruct((n_layers, batch, hd, RET_DK, RET_DV), F32)],
        scratch_shapes=[pltpu.VMEM((RET_DK, RET_DV), F32)],
        input_output_aliases=aliases,
        compiler_params=_params("parallel", "arbitrary"),
        name="ret_prompt")(*args)


def _ret_sample_kernel(sdec_ref, p_ref, s_ref, dm_ref, qd_ref, kd_ref, *rest, seq):
    o_ref, s_out_ref, y_ref = rest[-3:]
    pf = p_ref[...].astype(F32)
    for gi in range(SAMPLE_GROUP):
        rows = slice(gi * seq, (gi + 1) * seq)
        for h in range(RET_HEADS):
            qb = pf[rows, h * RET_DK:(h + 1) * RET_DK].astype(BF16)
            k = pf[rows, RET_QK + h * RET_DK:RET_QK + (h + 1) * RET_DK]
            vb = pf[rows, 2 * RET_QK + h * RET_DV:2 * RET_QK + (h + 1) * RET_DV].astype(BF16)
            s = s_ref[gi, h]
            p = (_dot_nt(qb, k.astype(BF16)) * dm_ref[h]).astype(BF16)
            y_ref[rows, h * RET_DV:(h + 1) * RET_DV] = (
                _dot(p, vb) + _dot(qb, s.astype(BF16)) * qd_ref[h])
            kdk = (k * kd_ref[h]).astype(BF16)
            s_out_ref[gi, h] = s * sdec_ref[h] + _dot_tn(kdk, vb)
    g0 = 2 * RET_QK + RET_V
    for h in range(RET_HEADS):
        cols = slice(h * RET_DV, (h + 1) * RET_DV)
        g = pf[:, g0 + h * RET_DV:g0 + (h + 1) * RET_DV]
        o_ref[:, cols] = _group_norm_gate(y_ref[:, cols], g).astype(o_ref.dtype)


def _ret_sample(ps, gated_small, state_all, layer, new_state_all, tabs, batch, seq):
    gsz = SAMPLE_GROUP
    rows = gsz * seq
    hd = RET_HEADS
    kern = functools.partial(_ret_sample_kernel, seq=seq)
    state_spec = pl.BlockSpec((None, gsz, hd, RET_DK, RET_DV), lambda b: (layer, b, 0, 0, 0))
    in_specs = [pl.BlockSpec(memory_space=pltpu.SMEM),
                pl.BlockSpec((rows, ps.shape[1]), lambda b: (b, 0)),
                state_spec,
                pl.BlockSpec((hd, seq, seq), lambda b: (0, 0, 0)),
                pl.BlockSpec((hd, seq, 1), lambda b: (0, 0, 0)),
                pl.BlockSpec((hd, seq, 1), lambda b: (0, 0, 0)),
                pl.BlockSpec(memory_space=pl.ANY)]
    args = [tabs["sdec_s"], ps, state_all, tabs["dm_s"], tabs["qd_s"], tabs["kd_s"], gated_small]
    aliases = {len(args) - 1: 0}
    if new_state_all is not None:
        in_specs.append(pl.BlockSpec(memory_space=pl.ANY))
        args.append(new_state_all)
        aliases[len(args) - 1] = 1
    return pl.pallas_call(
        kern, grid=(batch // gsz,),
        in_specs=in_specs,
        out_specs=[pl.BlockSpec((rows, RET_V), lambda b: (b, 0)), state_spec],
        out_shape=[jax.ShapeDtypeStruct(gated_small.shape, BF16),
                   jax.ShapeDtypeStruct(state_all.shape, F32)],
        scratch_shapes=[pltpu.VMEM((rows, RET_V), F32)],
        input_output_aliases=aliases,
        compiler_params=_params("parallel"),
        name="ret_sample")(*args)


def _gated_conv(u, zb, cw_ref, prev1, prev2, t):
    u1 = jnp.where(t >= 1, pltpu.roll(u, 1, 0), prev1)
    u2 = jnp.where(t >= 2, pltpu.roll(u, 2, 0), jnp.where(t == 1, prev1, prev2))
    conv = cw_ref[0:1, :] * u2 + cw_ref[1:2, :] * u1 + cw_ref[2:3, :] * u
    return zb * conv


def _conv_kernel(xm_ref, xs_ref, wb_ref, wc_ref, wv_ref, wz_ref, cw_ref, st_ref,
                 om_ref, os_ref, stp_ref, sts_ref, wbf_ref, meta_tail_ref, tail_ref, *,
                 n_sample, sample_seq, n_prompt, tiles_per_seq):
    t_step = pl.program_id(1)
    tn = om_ref.shape[1]
    n_sample_rows = n_sample * sample_seq

    def project(x):
        u = _dot(x, wbf_ref[1]) * _dot(x, wbf_ref[2])
        zb = _silu(_dot(x, wbf_ref[3])) * _dot(x, wbf_ref[0])
        return u, zb

    @pl.when(t_step == 0)
    def _():
        wbf_ref[0] = wb_ref[...].astype(BF16)
        wbf_ref[1] = wc_ref[...].astype(BF16)
        wbf_ref[2] = wv_ref[...].astype(BF16)
        wbf_ref[3] = wz_ref[...].astype(BF16)
        u, zb = project(xs_ref[...])
        us = u[0:n_sample_rows]
        st = st_ref[...]
        shape3 = (n_sample, sample_seq, tn)
        prev2 = jnp.broadcast_to(st[:, 0:1, :], shape3).reshape(n_sample_rows, tn)
        prev1 = jnp.broadcast_to(st[:, 1:2, :], shape3).reshape(n_sample_rows, tn)
        t = lax.broadcasted_iota(jnp.int32, us.shape, 0) % sample_seq
        os_ref[0:n_sample_rows, :] = _gated_conv(us, zb[0:n_sample_rows], cw_ref, prev1, prev2,
                                                 t).astype(os_ref.dtype)
        sts_ref[...] = us.reshape(shape3)[:, sample_seq - (CONV_WIDTH - 1):sample_seq, :]
        um = u[n_sample_rows:]
        t = lax.broadcasted_iota(jnp.int32, um.shape, 0) % N_META
        zero = jnp.zeros((), F32)
        os_ref[n_sample_rows:, :] = _gated_conv(um, zb[n_sample_rows:], cw_ref, zero, zero,
                                                t).astype(os_ref.dtype)
        for bb in range(n_prompt):
            meta_tail_ref[bb] = um[(bb + 1) * N_META - SUBLANES:(bb + 1) * N_META]

    @pl.when(t_step > 0)
    def _():
        tile = t_step - 1
        bb = tile // tiles_per_seq
        first = tile % tiles_per_seq == 0
        u, zb = project(xm_ref[...])
        rows = u.shape[0]
        tail = jnp.where(first, meta_tail_ref[bb], tail_ref[...])
        prev1 = tail[SUBLANES - 1:SUBLANES, :]
        prev2 = tail[SUBLANES - 2:SUBLANES - 1, :]
        t = lax.broadcasted_iota(jnp.int32, u.shape, 0)
        om_ref[...] = _gated_conv(u, zb, cw_ref, prev1, prev2, t).astype(om_ref.dtype)
        tail_ref[...] = u[rows - SUBLANES:rows]

        @pl.when(tile % tiles_per_seq == tiles_per_seq - 1)
        def _():
            stp_ref[0] = u[rows - (CONV_WIDTH - 1):rows]


def _conv_mixer(xm, xs, w_all, cw_all, layer, state_all, n_prompt, seq, n_sample, sample_seq):
    rm, k = xm.shape
    rs = xs.shape[0]
    e = CONV_E
    tn, tm = CONV_TN, MAIN_TILE
    nt = e // tn
    n_main = rm // tm
    tiles_per_seq = seq // tm
    ns = CONV_WIDTH - 1

    def main_tile(t):
        return jnp.maximum(t - 1, 0)

    w_specs = [pl.BlockSpec((None, k, tn), functools.partial(lambda j, t, q: (layer, 0, q * nt + j), q=q))
               for q in range(4)]
    kern = functools.partial(_conv_kernel, n_sample=n_sample, sample_seq=sample_seq, n_prompt=n_prompt,
                             tiles_per_seq=tiles_per_seq)
    return pl.pallas_call(
        kern, grid=(nt, n_main + 1),
        in_specs=[pl.BlockSpec((tm, k), lambda j, t: (main_tile(t), 0)),
                  pl.BlockSpec((rs, k), lambda j, t: (0, 0))] + w_specs + [
                  pl.BlockSpec((None, CONV_WIDTH, tn), lambda j, t: (layer, 0, j)),
                  pl.BlockSpec((None, n_sample, ns, tn), lambda j, t: (layer, 0, 0, j))],
        out_specs=[pl.BlockSpec((tm, tn), lambda j, t: (main_tile(t), j)),
                   pl.BlockSpec((rs, tn), lambda j, t: (0, j)),
                   pl.BlockSpec((1, ns, tn), lambda j, t: (main_tile(t) // tiles_per_seq, 0, j)),
                   pl.BlockSpec((n_sample, ns, tn), lambda j, t: (0, 0, j))],
        out_shape=[jax.ShapeDtypeStruct((rm, e), BF16),
                   jax.ShapeDtypeStruct((rs, e), BF16),
                   jax.ShapeDtypeStruct((n_prompt, ns, e), F32),
                   jax.ShapeDtypeStruct((n_sample, ns, e), F32)],
        scratch_shapes=[pltpu.VMEM((4, k, tn), BF16),
                        pltpu.VMEM((n_prompt, SUBLANES, tn), F32),
                        pltpu.VMEM((SUBLANES, tn), F32)],
        compiler_params=_params("parallel", "arbitrary"),
        name="conv_mixer")(xm, xs, w_all, w_all, w_all, w_all, cw_all, state_all)


def _rope_tables(pos):
    freqs = ROPE_BASE ** (-jnp.arange(HALF, dtype=F32) * 2.0 / RET_DK)
    ang = pos.astype(F32)[:, None] * freqs[None, :]
    return jnp.cos(ang), jnp.sin(ang)


def _decay_tables(length):
    lg = jnp.log1p(-jnp.exp2(-5.0 - jnp.arange(RET_HEADS, dtype=F32)))
    idx = jnp.arange(length, dtype=F32)
    diff = idx[:, None] - idx[None, :]
    dmask = jnp.exp(jnp.where(diff[None] >= 0, diff[None] * lg[:, None, None], -jnp.inf))
    q_decay = jnp.exp((idx + 1.0)[None, :] * lg[:, None])[:, :, None]
    k_decay = jnp.exp((length - 1.0 - idx)[None, :] * lg[:, None])[:, :, None]
    s_decay = jnp.exp(length * lg)
    return dmask, q_decay, k_decay, s_decay


def kernel(x_prompt, x_sample, state_ret, state_conv, meta, w_ret_in, w_ret_out, w_conv_in, conv_w,
           w_conv_out, ln_g, ln_b):
    bp, lp, d = x_prompt.shape
    bs, ls, _ = x_sample.shape
    n_sample_rows = bs * ls
    n_ret = state_ret.shape[0]

    xm = x_prompt.reshape(bp * lp, d)
    xs = jnp.concatenate([x_sample.reshape(n_sample_rows, d),
                          jnp.tile(meta.astype(x_prompt.dtype), (bp, 1))], axis=0)
    xm_bf, xs_bf = xm.astype(BF16), xs.astype(BF16)

    rope_m = _rope_tables(N_META + jnp.arange(lp, dtype=jnp.int32))
    pos_s = jnp.concatenate([jnp.tile(PAST_LEN + jnp.arange(ls, dtype=jnp.int32), bs),
                             jnp.tile(jnp.arange(N_META, dtype=jnp.int32), bp)])
    rope_s = _rope_tables(pos_s)

    dm_p, qd_p, kd_p, sdec_p = _decay_tables(PROMPT_CHUNK)
    _, _, kd_m, _ = _decay_tables(N_META)
    dm_s, qd_s, kd_s, sdec_s = _decay_tables(ls)
    tabs = dict(dm_p=dm_p, qd_p=qd_p, kd_p=kd_p, sdec_p=sdec_p, kd_m=kd_m,
                dm_s=dm_s, qd_s=qd_s, kd_s=kd_s, sdec_s=sdec_s)

    w_ret_out_bf = w_ret_out.astype(BF16)
    w_conv_out_bf = w_conv_out.astype(BF16)

    ret_p = ret_s = None
    conv_p, conv_s = [], []
    for i in range(DEPTH):
        j = i // 2
        g_i, b_i = ln_g[i][None, :], ln_b[i][None, :]
        if i % 2 == 0:
            pm, ps = _ret_proj(xm_bf, xs_bf, w_ret_in, j, rope_m, rope_s, lp)
            am, a_small, ret_p = _ret_prompt(pm, ps, j, ret_p, n_ret, tabs, bp, lp, n_sample_rows)
            a_small, ret_s = _ret_sample(ps, a_small, state_ret, j, ret_s, tabs, bs, ls)
            w_out = w_ret_out_bf
        else:
            am, a_small, sp, ss = _conv_mixer(xm_bf, xs_bf, w_conv_in, conv_w, j, state_conv, bp, lp, bs, ls)
            conv_p.append(sp)
            conv_s.append(ss)
            w_out = w_conv_out_bf
        xm, xm_bf = _out_ln(am, w_out, j, xm, g_i, b_i, OUT_TM_MAIN)
        xs, xs_bf = _out_ln(a_small, w_out, j, xs, g_i, b_i, xs.shape[0] // 2)

    y_prompt = xm.reshape(bp, lp, d)
    y_sample = xs[:n_sample_rows].reshape(bs, ls, d)
    return (y_prompt, y_sample, ret_p, ret_s, jnp.stack(conv_p), jnp.stack(conv_s))
```

```python
import functools

import jax
import jax.numpy as jnp
from jax import lax
from jax.experimental import pallas as pl
from jax.experimental.pallas import tpu as pltpu

D_MODEL = 2048
DEPTH = 4
PAST_LEN = 16384
N_META = 16
RET_HEADS = 8
RET_DK = D_MODEL // RET_HEADS
RET_DV = 2 * RET_DK
RET_QK = RET_HEADS * RET_DK
RET_V = RET_HEADS * RET_DV
CONV_WIDTH = 3
CONV_E = D_MODEL
ROPE_BASE = 10000.0
LN_EPS = 1e-5
GN_EPS = 1e-6
DEEPNORM_ALPHA = (2.0 * DEPTH) ** 0.25

HALF = RET_DK // 2
SUBLANES = 8
PROMPT_CHUNK = 256
SAMPLE_GROUP = 2
MAIN_TILE = 1024
PROJ_TN = 1024
CONV_TN = 256
PROJ_SUBBLOCKS = 4
OUT_TM = 512
OUT_SUB = 256
VMEM_LIMIT = 56 * 1024 * 1024

BF16 = jnp.bfloat16
F32 = jnp.float32


def _params(*sem):
    return pltpu.CompilerParams(dimension_semantics=sem, vmem_limit_bytes=VMEM_LIMIT)


def _dot(a, b):
    return jnp.dot(a, b, preferred_element_type=F32)


def _dot_nt(a, b):
    return lax.dot_general(a, b, (((1,), (1,)), ((), ())), preferred_element_type=F32)


def _dot_tn(a, b):
    return lax.dot_general(a, b, (((0,), (0,)), ((), ())), preferred_element_type=F32)


def _silu(x):
    return x * (1.0 / (1.0 + jnp.exp(-x)))


def _proj_kernel(xm_ref, xs_ref, w_ref, cm_ref, sm_ref, cs_ref, ss_ref, om_ref, os_ref, wbf_ref, *,
                 n_main, tn):
    j = pl.program_id(0)
    i = pl.program_id(1)
    rope_tiles = 2 * RET_QK // tn
    q_tiles = RET_QK // tn

    @pl.when(i == 0)
    def _():
        wbf_ref[...] = w_ref[...].astype(BF16)

    is_rope = j < rope_tiles
    scale = jnp.where(j < q_tiles, RET_DK ** -0.5, 1.0).astype(F32)

    def path(x_ref, c_ref, s_ref, o_ref):
        sub = x_ref.shape[0] // PROJ_SUBBLOCKS
        for r in range(PROJ_SUBBLOCKS):
            rows = slice(r * sub, (r + 1) * sub)
            acc = _dot(x_ref[rows, :], wbf_ref[...])
            c = c_ref[rows, :] * scale
            s = s_ref[rows, :] * scale
            for h in range(tn // RET_DK):
                lo = slice(h * RET_DK, h * RET_DK + HALF)
                hi = slice(h * RET_DK + HALF, (h + 1) * RET_DK)
                t1 = acc[:, lo]
                t2 = acc[:, hi]
                o_ref[rows, lo] = jnp.where(is_rope, t1 * c - t2 * s, t1).astype(o_ref.dtype)
                o_ref[rows, hi] = jnp.where(is_rope, t1 * s + t2 * c, t2).astype(o_ref.dtype)

    @pl.when(i < n_main)
    def _():
        path(xm_ref, cm_ref, sm_ref, om_ref)

    @pl.when(i == n_main)
    def _():
        path(xs_ref, cs_ref, ss_ref, os_ref)


def _ret_proj(xm, xs, w_all, layer, rope_m, rope_s, seq):
    rm, k = xm.shape
    rs = xs.shape[0]
    n = w_all.shape[2]
    tm, tn = MAIN_TILE, PROJ_TN
    n_main = rm // tm
    tiles_per_seq = seq // tm
    last = n_main - 1
    kern = functools.partial(_proj_kernel, n_main=n_main, tn=tn)
    return pl.pallas_call(
        kern, grid=(n // tn, n_main + 1),
        in_specs=[pl.BlockSpec((tm, k), lambda j, i: (jnp.minimum(i, last), 0)),
                  pl.BlockSpec((rs, k), lambda j, i: (0, 0)),
                  pl.BlockSpec((None, k, tn), lambda j, i: (layer, 0, j)),
                  pl.BlockSpec((tm, HALF), lambda j, i: (jnp.minimum(i, last) % tiles_per_seq, 0)),
                  pl.BlockSpec((tm, HALF), lambda j, i: (jnp.minimum(i, last) % tiles_per_seq, 0)),
                  pl.BlockSpec((rs, HALF), lambda j, i: (0, 0)),
                  pl.BlockSpec((rs, HALF), lambda j, i: (0, 0))],
        out_specs=[pl.BlockSpec((tm, tn), lambda j, i: (jnp.minimum(i, last), j)),
                   pl.BlockSpec((rs, tn), lambda j, i: (0, j))],
        out_shape=[jax.ShapeDtypeStruct((rm, n), BF16), jax.ShapeDtypeStruct((rs, n), BF16)],
        scratch_shapes=[pltpu.VMEM((k, tn), BF16)],
        compiler_params=_params("parallel", "arbitrary"),
        name="ret_proj")(xm, xs, w_all, rope_m[0], rope_m[1], rope_s[0], rope_s[1])


def _out_ln_kernel(a_ref, w_ref, x_ref, g_ref, b_ref, y_ref, ybf_ref, *, sub):
    for r in range(a_ref.shape[0] // sub):
        rows = slice(r * sub, (r + 1) * sub)
        y = DEEPNORM_ALPHA * x_ref[rows, :] + _dot(a_ref[rows, :], w_ref[...])
        mu = jnp.mean(y, axis=-1, keepdims=True)
        d = y - mu
        var = jnp.mean(d * d, axis=-1, keepdims=True)
        out = d * lax.rsqrt(var + LN_EPS) * g_ref[...] + b_ref[...]
        y_ref[rows, :] = out
        ybf_ref[rows, :] = out.astype(BF16)


def _out_ln(a, w_all, layer, x, g, b, tm, sub):
    m, k = a.shape
    n = w_all.shape[2]
    kern = functools.partial(_out_ln_kernel, sub=sub)
    return pl.pallas_call(
        kern, grid=(m // tm,),
        in_specs=[pl.BlockSpec((tm, k), lambda i: (i, 0)),
                  pl.BlockSpec((None, k, n), lambda i: (layer, 0, 0), pipeline_mode=pl.Buffered(1)),
                  pl.BlockSpec((tm, n), lambda i: (i, 0)),
                  pl.BlockSpec((1, n), lambda i: (0, 0)),
                  pl.BlockSpec((1, n), lambda i: (0, 0))],
        out_specs=[pl.BlockSpec((tm, n), lambda i: (i, 0)),
                   pl.BlockSpec((tm, n), lambda i: (i, 0))],
        out_shape=[jax.ShapeDtypeStruct((m, n), F32), jax.ShapeDtypeStruct((m, n), BF16)],
        compiler_params=_params("parallel"),
        name="out_ln")(a, w_all, x, g, b)


def _group_norm_gate(y, g):
    mu = jnp.mean(y, axis=-1, keepdims=True)
    d = y - mu
    var = jnp.mean(d * d, axis=-1, keepdims=True)
    return _silu(g.astype(F32)) * (d * lax.rsqrt(var + GN_EPS))


def _ret_prompt_kernel(sdec_ref, q_ref, k_ref, v_ref, g_ref, qm_ref, km_ref, vm_ref, gm_ref,
                       dm_ref, qd_ref, kd_ref, kdm_ref, *rest, n_chunks, n_sample_rows):
    o_ref, os_ref, s_out_ref, s_ref = rest[-4:]
    h = pl.program_id(0)
    b = pl.program_id(1)
    c_len = PROMPT_CHUNK

    @pl.when(b == 0)
    def _():
        os_ref[0:n_sample_rows, :] = jnp.zeros((n_sample_rows, RET_DV), os_ref.dtype)

    qm = qm_ref[...]
    km = km_ref[...]
    vm = vm_ref[...]
    p = (_dot_nt(qm, km) * dm_ref[0, 0:N_META, 0:N_META]).astype(BF16)
    r_meta = pl.multiple_of(n_sample_rows + b * N_META, N_META)
    os_ref[pl.ds(r_meta, N_META), :] = _group_norm_gate(_dot(p, vm), gm_ref[...]).astype(os_ref.dtype)
    kdk = (km.astype(F32) * kdm_ref[0]).astype(BF16)
    s_ref[...] = _dot_tn(kdk, vm)

    sd = sdec_ref[h]

    def chunk(c, carry):
        r0 = pl.multiple_of(c * c_len, c_len)
        qc = q_ref[pl.ds(r0, c_len), :]
        kc = k_ref[pl.ds(r0, c_len), :]
        vc = v_ref[pl.ds(r0, c_len), :]
        s = s_ref[...]
        p = (_dot_nt(qc, kc) * dm_ref[0]).astype(BF16)
        y = _dot(p, vc) + _dot(qc, s.astype(BF16)) * qd_ref[0]
        o_ref[pl.ds(r0, c_len), :] = _group_norm_gate(y, g_ref[pl.ds(r0, c_len), :]).astype(o_ref.dtype)
        kdk = (kc.astype(F32) * kd_ref[0]).astype(BF16)
        s_ref[...] = s * sd + _dot_tn(kdk, vc)
        return carry

    lax.fori_loop(0, n_chunks, chunk, 0, unroll=True)
    s_out_ref[...] = s_ref[...]


def _ret_prompt(pm, ps, layer, new_state_all, n_layers, tabs, batch, seq, n_sample_rows):
    c_len = PROMPT_CHUNK
    hd = RET_HEADS
    rs = ps.shape[0]
    meta0 = n_sample_rows // N_META
    qk_blocks = RET_QK // RET_DK
    v_blocks = 2 * RET_QK // RET_DV
    g_blocks = v_blocks + RET_V // RET_DV
    kern = functools.partial(_ret_prompt_kernel, n_chunks=seq // c_len, n_sample_rows=n_sample_rows)
    state_spec = pl.BlockSpec((None, None, None, RET_DK, RET_DV), lambda h, b: (layer, b, h, 0, 0))
    in_specs = [pl.BlockSpec(memory_space=pltpu.SMEM),
                pl.BlockSpec((seq, RET_DK), lambda h, b: (b, h)),
                pl.BlockSpec((seq, RET_DK), lambda h, b: (b, qk_blocks + h)),
                pl.BlockSpec((seq, RET_DV), lambda h, b: (b, v_blocks + h)),
                pl.BlockSpec((seq, RET_DV), lambda h, b: (b, g_blocks + h)),
                pl.BlockSpec((N_META, RET_DK), lambda h, b: (meta0 + b, h)),
                pl.BlockSpec((N_META, RET_DK), lambda h, b: (meta0 + b, qk_blocks + h)),
                pl.BlockSpec((N_META, RET_DV), lambda h, b: (meta0 + b, v_blocks + h)),
                pl.BlockSpec((N_META, RET_DV), lambda h, b: (meta0 + b, g_blocks + h)),
                pl.BlockSpec((1, c_len, c_len), lambda h, b: (h, 0, 0)),
                pl.BlockSpec((1, c_len, 1), lambda h, b: (h, 0, 0)),
                pl.BlockSpec((1, c_len, 1), lambda h, b: (h, 0, 0)),
                pl.BlockSpec((1, N_META, 1), lambda h, b: (h, 0, 0))]
    args = [tabs["sdec_p"], pm, pm, pm, pm, ps, ps, ps, ps, tabs["dm_p"], tabs["qd_p"], tabs["kd_p"],
            tabs["kd_m"]]
    aliases = {}
    if new_state_all is not None:
        in_specs.append(pl.BlockSpec(memory_space=pl.ANY))
        args.append(new_state_all)
        aliases = {len(args) - 1: 2}
    return pl.pallas_call(
        kern, grid=(hd, batch),
        in_specs=in_specs,
        out_specs=[pl.BlockSpec((seq, RET_DV), lambda h, b: (b, h)),
                   pl.BlockSpec((rs, RET_DV), lambda h, b: (0, h)),
                   state_spec],
        out_shape=[jax.ShapeDtypeStruct((batch * seq, RET_V), BF16),
                   jax.ShapeDtypeStruct((rs, RET_V), BF16),
                   jax.ShapeDtypeStruct((n_layers, batch, hd, RET_DK, RET_DV), F32)],
        scratch_shapes=[pltpu.VMEM((RET_DK, RET_DV), F32)],
        input_output_aliases=aliases,
        compiler_params=_params("parallel", "arbitrary"),
        name="ret_prompt")(*args)


def _ret_sample_kernel(sdec_ref, p_ref, s_ref, dm_ref, qd_ref, kd_ref, *rest, seq):
    o_ref, s_out_ref, y_ref = rest[-3:]
    pf = p_ref[...].astype(F32)
    for gi in range(SAMPLE_GROUP):
        rows = slice(gi * seq, (gi + 1) * seq)
        for h in range(RET_HEADS):
            qb = pf[rows, h * RET_DK:(h + 1) * RET_DK].astype(BF16)
            k = pf[rows, RET_QK + h * RET_DK:RET_QK + (h + 1) * RET_DK]
            vb = pf[rows, 2 * RET_QK + h * RET_DV:2 * RET_QK + (h + 1) * RET_DV].astype(BF16)
            s = s_ref[gi, h]
            p = (_dot_nt(qb, k.astype(BF16)) * dm_ref[h]).astype(BF16)
            y_ref[rows, h * RET_DV:(h + 1) * RET_DV] = (
                _dot(p, vb) + _dot(qb, s.astype(BF16)) * qd_ref[h])
            kdk = (k * kd_ref[h]).astype(BF16)
            s_out_ref[gi, h] = s * sdec_ref[h] + _dot_tn(kdk, vb)
    g0 = 2 * RET_QK + RET_V
    for h in range(RET_HEADS):
        cols = slice(h * RET_DV, (h + 1) * RET_DV)
        g = pf[:, g0 + h * RET_DV:g0 + (h + 1) * RET_DV]
        o_ref[:, cols] = _group_norm_gate(y_ref[:, cols], g).astype(o_ref.dtype)


def _ret_sample(ps, gated_small, state_all, layer, new_state_all, tabs, batch, seq):
    gsz = SAMPLE_GROUP
    rows = gsz * seq
    hd = RET_HEADS
    kern = functools.partial(_ret_sample_kernel, seq=seq)
    state_spec = pl.BlockSpec((None, gsz, hd, RET_DK, RET_DV), lambda b: (layer, b, 0, 0, 0))
    in_specs = [pl.BlockSpec(memory_space=pltpu.SMEM),
                pl.BlockSpec((rows, ps.shape[1]), lambda b: (b, 0)),
                state_spec,
                pl.BlockSpec((hd, seq, seq), lambda b: (0, 0, 0)),
                pl.BlockSpec((hd, seq, 1), lambda b: (0, 0, 0)),
                pl.BlockSpec((hd, seq, 1), lambda b: (0, 0, 0)),
                pl.BlockSpec(memory_space=pl.ANY)]
    args = [tabs["sdec_s"], ps, state_all, tabs["dm_s"], tabs["qd_s"], tabs["kd_s"], gated_small]
    aliases = {len(args) - 1: 0}
    if new_state_all is not None:
        in_specs.append(pl.BlockSpec(memory_space=pl.ANY))
        args.append(new_state_all)
        aliases[len(args) - 1] = 1
    return pl.pallas_call(
        kern, grid=(batch // gsz,),
        in_specs=in_specs,
        out_specs=[pl.BlockSpec((rows, RET_V), lambda b: (b, 0)), state_spec],
        out_shape=[jax.ShapeDtypeStruct(gated_small.shape, BF16),
                   jax.ShapeDtypeStruct(state_all.shape, F32)],
        scratch_shapes=[pltpu.VMEM((rows, RET_V), F32)],
        input_output_aliases=aliases,
        compiler_params=_params("parallel"),
        name="ret_sample")(*args)


def _gated_conv(u, zb, cw_ref, prev1, prev2, t):
    u1 = jnp.where(t >= 1, pltpu.roll(u, 1, 0), prev1)
    u2 = jnp.where(t >= 2, pltpu.roll(u, 2, 0), jnp.where(t == 1, prev1, prev2))
    conv = cw_ref[0:1, :] * u2 + cw_ref[1:2, :] * u1 + cw_ref[2:3, :] * u
    return zb * conv


def _conv_kernel(xm_ref, xs_ref, wb_ref, wc_ref, wv_ref, wz_ref, cw_ref, st_ref,
                 om_ref, os_ref, stp_ref, sts_ref, wbf_ref, meta_tail_ref, tail_ref, *,
                 n_sample, sample_seq, n_prompt, tiles_per_seq):
    t_step = pl.program_id(1)
    tn = om_ref.shape[1]
    n_sample_rows = n_sample * sample_seq

    def project(x):
        u = _dot(x, wbf_ref[1]) * _dot(x, wbf_ref[2])
        zb = _silu(_dot(x, wbf_ref[3])) * _dot(x, wbf_ref[0])
        return u, zb

    @pl.when(t_step == 0)
    def _():
        wbf_ref[0] = wb_ref[...].astype(BF16)
        wbf_ref[1] = wc_ref[...].astype(BF16)
        wbf_ref[2] = wv_ref[...].astype(BF16)
        wbf_ref[3] = wz_ref[...].astype(BF16)
        u, zb = project(xs_ref[...])
        us = u[0:n_sample_rows]
        st = st_ref[...]
        shape3 = (n_sample, sample_seq, tn)
        prev2 = jnp.broadcast_to(st[:, 0:1, :], shape3).reshape(n_sample_rows, tn)
        prev1 = jnp.broadcast_to(st[:, 1:2, :], shape3).reshape(n_sample_rows, tn)
        t = lax.broadcasted_iota(jnp.int32, us.shape, 0) % sample_seq
        os_ref[0:n_sample_rows, :] = _gated_conv(us, zb[0:n_sample_rows], cw_ref, prev1, prev2,
                                                 t).astype(os_ref.dtype)
        sts_ref[...] = us.reshape(shape3)[:, sample_seq - (CONV_WIDTH - 1):sample_seq, :]
        um = u[n_sample_rows:]
        t = lax.broadcasted_iota(jnp.int32, um.shape, 0) % N_META
        zero = jnp.zeros((), F32)
        os_ref[n_sample_rows:, :] = _gated_conv(um, zb[n_sample_rows:], cw_ref, zero, zero,
                                                t).astype(os_ref.dtype)
        for bb in range(n_prompt):
            meta_tail_ref[bb] = um[(bb + 1) * N_META - SUBLANES:(bb + 1) * N_META]

    @pl.when(t_step > 0)
    def _():
        tile = t_step - 1
        bb = tile // tiles_per_seq
        first = tile % tiles_per_seq == 0
        u, zb = project(xm_ref[...])
        rows = u.shape[0]
        tail = jnp.where(first, meta_tail_ref[bb], tail_ref[...])
        prev1 = tail[SUBLANES - 1:SUBLANES, :]
        prev2 = tail[SUBLANES - 2:SUBLANES - 1, :]
        t = lax.broadcasted_iota(jnp.int32, u.shape, 0)
        om_ref[...] = _gated_conv(u, zb, cw_ref, prev1, prev2, t).astype(om_ref.dtype)
        tail_ref[...] = u[rows - SUBLANES:rows]

        @pl.when(tile % tiles_per_seq == tiles_per_seq - 1)
        def _():
            stp_ref[0] = u[rows - (CONV_WIDTH - 1):rows]


def _conv_mixer(xm, xs, w_all, cw_all, layer, state_all, n_prompt, seq, n_sample, sample_seq):
    rm, k = xm.shape
    rs = xs.shape[0]
    e = CONV_E
    tn, tm = CONV_TN, MAIN_TILE
    nt = e // tn
    n_main = rm // tm
    tiles_per_seq = seq // tm
    ns = CONV_WIDTH - 1

    def main_tile(t):
        return jnp.maximum(t - 1, 0)

    w_specs = [pl.BlockSpec((None, k, tn), functools.partial(lambda j, t, q: (layer, 0, q * nt + j), q=q))
               for q in range(4)]
    kern = functools.partial(_conv_kernel, n_sample=n_sample, sample_seq=sample_seq, n_prompt=n_prompt,
                             tiles_per_seq=tiles_per_seq)
    return pl.pallas_call(
        kern, grid=(nt, n_main + 1),
        in_specs=[pl.BlockSpec((tm, k), lambda j, t: (main_tile(t), 0)),
                  pl.BlockSpec((rs, k), lambda j, t: (0, 0))] + w_specs + [
                  pl.BlockSpec((None, CONV_WIDTH, tn), lambda j, t: (layer, 0, j)),
                  pl.BlockSpec((None, n_sample, ns, tn), lambda j, t: (layer, 0, 0, j))],
        out_specs=[pl.BlockSpec((tm, tn), lambda j, t: (main_tile(t), j)),
                   pl.BlockSpec((rs, tn), lambda j, t: (0, j)),
                   pl.BlockSpec((1, ns, tn), lambda j, t: (main_tile(t) // tiles_per_seq, 0, j)),
                   pl.BlockSpec((n_sample, ns, tn), lambda j, t: (0, 0, j))],
        out_shape=[jax.ShapeDtypeStruct((rm, e), BF16),
                   jax.ShapeDtypeStruct((rs, e), BF16),
                   jax.ShapeDtypeStruct((n_prompt, ns, e), F32),
                   jax.ShapeDtypeStruct((n_sample, ns, e), F32)],
        scratch_shapes=[pltpu.VMEM((4, k, tn), BF16),
                        pltpu.VMEM((n_prompt, SUBLANES, tn), F32),
                        pltpu.VMEM((SUBLANES, tn), F32)],
        compiler_params=_params("parallel", "arbitrary"),
        name="conv_mixer")(xm, xs, w_all, w_all, w_all, w_all, cw_all, state_all)


def _rope_tables(pos):
    freqs = ROPE_BASE ** (-jnp.arange(HALF, dtype=F32) * 2.0 / RET_DK)
    ang = pos.astype(F32)[:, None] * freqs[None, :]
    return jnp.cos(ang), jnp.sin(ang)


def _decay_tables(length):
    lg = jnp.log1p(-jnp.exp2(-5.0 - jnp.arange(RET_HEADS, dtype=F32)))
    idx = jnp.arange(length, dtype=F32)
    diff = idx[:, None] - idx[None, :]
    dmask = jnp.exp(jnp.where(diff[None] >= 0, diff[None] * lg[:, None, None], -jnp.inf))
    q_decay = jnp.exp((idx + 1.0)[None, :] * lg[:, None])[:, :, None]
    k_decay = jnp.exp((length - 1.0 - idx)[None, :] * lg[:, None])[:, :, None]
    s_decay = jnp.exp(length * lg)
    return dmask, q_decay, k_decay, s_decay


def kernel(x_prompt, x_sample, state_ret, state_conv, meta, w_ret_in, w_ret_out, w_conv_in, conv_w,
           w_conv_out, ln_g, ln_b):
    bp, lp, d = x_prompt.shape
    bs, ls, _ = x_sample.shape
    n_sample_rows = bs * ls
    n_ret = state_ret.shape[0]

    xm = x_prompt.reshape(bp * lp, d)
    xs = jnp.concatenate([x_sample.reshape(n_sample_rows, d),
                          jnp.tile(meta.astype(x_prompt.dtype), (bp, 1))], axis=0)
    xm_bf, xs_bf = xm.astype(BF16), xs.astype(BF16)

    rope_m = _rope_tables(N_META + jnp.arange(lp, dtype=jnp.int32))
    pos_s = jnp.concatenate([jnp.tile(PAST_LEN + jnp.arange(ls, dtype=jnp.int32), bs),
                             jnp.tile(jnp.arange(N_META, dtype=jnp.int32), bp)])
    rope_s = _rope_tables(pos_s)

    dm_p, qd_p, kd_p, sdec_p = _decay_tables(PROMPT_CHUNK)
    _, _, kd_m, _ = _decay_tables(N_META)
    dm_s, qd_s, kd_s, sdec_s = _decay_tables(ls)
    tabs = dict(dm_p=dm_p, qd_p=qd_p, kd_p=kd_p, sdec_p=sdec_p, kd_m=kd_m,
                dm_s=dm_s, qd_s=qd_s, kd_s=kd_s, sdec_s=sdec_s)

    w_ret_out_bf = w_ret_out.astype(BF16)
    w_conv_out_bf = w_conv_out.astype(BF16)

    ret_p = ret_s = None
    conv_p, conv_s = [], []
    for i in range(DEPTH):
        j = i // 2
        g_i, b_i = ln_g[i][None, :], ln_b[i][None, :]
        if i % 2 == 0:
            pm, ps = _ret_proj(xm_bf, xs_bf, w_ret_in, j, rope_m, rope_s, lp)
            am, a_small, ret_p = _ret_prompt(pm, ps, j, ret_p, n_ret, tabs, bp, lp, n_sample_rows)
            a_small, ret_s = _ret_sample(ps, a_small, state_ret, j, ret_s, tabs, bs, ls)
            w_out = w_ret_out_bf
        else:
            am, a_small, sp, ss = _conv_mixer(xm_bf, xs_bf, w_conv_in, conv_w, j, state_conv, bp, lp, bs, ls)
            conv_p.append(sp)
            conv_s.append(ss)
            w_out = w_conv_out_bf
        xm, xm_bf = _out_ln(am, w_out, j, xm, g_i, b_i, OUT_TM, OUT_SUB)
        xs, xs_bf = _out_ln(a_small, w_out, j, xs, g_i, b_i, xs.shape[0] // 2, xs.shape[0] // 4)

    y_prompt = xm.reshape(bp, lp, d)
    y_sample = xs[:n_sample_rows].reshape(bs, ls, d)
    return (y_prompt, y_sample, ret_p, ret_s, jnp.stack(conv_p), jnp.stack(conv_s))
```

```python
import functools

import jax
import jax.numpy as jnp
from jax import lax
from jax.experimental import pallas as pl
from jax.experimental.pallas import tpu as pltpu

D_MODEL = 2048
DEPTH = 4
PAST_LEN = 16384
N_META = 16
RET_HEADS = 8
RET_DK = D_MODEL // RET_HEADS
RET_DV = 2 * RET_DK
RET_QK = RET_HEADS * RET_DK
RET_V = RET_HEADS * RET_DV
CONV_WIDTH = 3
CONV_E = D_MODEL
ROPE_BASE = 10000.0
LN_EPS = 1e-5
GN_EPS = 1e-6
DEEPNORM_ALPHA = (2.0 * DEPTH) ** 0.25

HALF = RET_DK // 2
SUBLANES = 8
PROMPT_CHUNK = 256
SAMPLE_GROUP = 2
RET_PARTS = 2
MAIN_TILE = 1024
PROJ_TN = 1024
CONV_TN = 256
PROJ_SUBBLOCKS = 4
OUT_TM = 512
OUT_SUB = 256
VMEM_LIMIT = 56 * 1024 * 1024

BF16 = jnp.bfloat16
F32 = jnp.float32


def _params(*sem):
    return pltpu.CompilerParams(dimension_semantics=sem, vmem_limit_bytes=VMEM_LIMIT)


def _dot(a, b):
    return jnp.dot(a, b, preferred_element_type=F32)


def _dot_nt(a, b):
    return lax.dot_general(a, b, (((1,), (1,)), ((), ())), preferred_element_type=F32)


def _dot_tn(a, b):
    return lax.dot_general(a, b, (((0,), (0,)), ((), ())), preferred_element_type=F32)


def _silu(x):
    return x * (1.0 / (1.0 + jnp.exp(-x)))


def _proj_kernel(xm_ref, xs_ref, w_ref, cm_ref, sm_ref, cs_ref, ss_ref, om_ref, os_ref, wbf_ref, *,
                 n_main, tn):
    j = pl.program_id(0)
    i = pl.program_id(1)
    rope_tiles = 2 * RET_QK // tn
    q_tiles = RET_QK // tn

    @pl.when(i == 0)
    def _():
        wbf_ref[...] = w_ref[...].astype(BF16)

    is_rope = j < rope_tiles
    scale = jnp.where(j < q_tiles, RET_DK ** -0.5, 1.0).astype(F32)

    def path(x_ref, c_ref, s_ref, o_ref):
        sub = x_ref.shape[0] // PROJ_SUBBLOCKS
        for r in range(PROJ_SUBBLOCKS):
            rows = slice(r * sub, (r + 1) * sub)
            acc = _dot(x_ref[rows, :], wbf_ref[...])
            c = c_ref[rows, :] * scale
            s = s_ref[rows, :] * scale
            for h in range(tn // RET_DK):
                lo = slice(h * RET_DK, h * RET_DK + HALF)
                hi = slice(h * RET_DK + HALF, (h + 1) * RET_DK)
                t1 = acc[:, lo]
                t2 = acc[:, hi]
                o_ref[rows, lo] = jnp.where(is_rope, t1 * c - t2 * s, t1).astype(o_ref.dtype)
                o_ref[rows, hi] = jnp.where(is_rope, t1 * s + t2 * c, t2).astype(o_ref.dtype)

    @pl.when(i < n_main)
    def _():
        path(xm_ref, cm_ref, sm_ref, om_ref)

    @pl.when(i == n_main)
    def _():
        path(xs_ref, cs_ref, ss_ref, os_ref)


def _ret_proj(xm, xs, w_all, layer, rope_m, rope_s, seq):
    rm, k = xm.shape
    rs = xs.shape[0]
    n = w_all.shape[2]
    tm, tn = MAIN_TILE, PROJ_TN
    n_main = rm // tm
    tiles_per_seq = seq // tm
    last = n_main - 1
    kern = functools.partial(_proj_kernel, n_main=n_main, tn=tn)
    return pl.pallas_call(
        kern, grid=(n // tn, n_main + 1),
        in_specs=[pl.BlockSpec((tm, k), lambda j, i: (jnp.minimum(i, last), 0)),
                  pl.BlockSpec((rs, k), lambda j, i: (0, 0)),
                  pl.BlockSpec((None, k, tn), lambda j, i: (layer, 0, j)),
                  pl.BlockSpec((tm, HALF), lambda j, i: (jnp.minimum(i, last) % tiles_per_seq, 0)),
                  pl.BlockSpec((tm, HALF), lambda j, i: (jnp.minimum(i, last) % tiles_per_seq, 0)),
                  pl.BlockSpec((rs, HALF), lambda j, i: (0, 0)),
                  pl.BlockSpec((rs, HALF), lambda j, i: (0, 0))],
        out_specs=[pl.BlockSpec((tm, tn), lambda j, i: (jnp.minimum(i, last), j)),
                   pl.BlockSpec((rs, tn), lambda j, i: (0, j))],
        out_shape=[jax.ShapeDtypeStruct((rm, n), BF16), jax.ShapeDtypeStruct((rs, n), BF16)],
        scratch_shapes=[pltpu.VMEM((k, tn), BF16)],
        compiler_params=_params("parallel", "arbitrary"),
        name="ret_proj")(xm, xs, w_all, rope_m[0], rope_m[1], rope_s[0], rope_s[1])


def _out_ln_kernel(a_ref, w_ref, x_ref, g_ref, b_ref, y_ref, ybf_ref, *, sub):
    for r in range(a_ref.shape[0] // sub):
        rows = slice(r * sub, (r + 1) * sub)
        y = DEEPNORM_ALPHA * x_ref[rows, :] + _dot(a_ref[rows, :], w_ref[...])
        mu = jnp.mean(y, axis=-1, keepdims=True)
        d = y - mu
        var = jnp.mean(d * d, axis=-1, keepdims=True)
        out = d * lax.rsqrt(var + LN_EPS) * g_ref[...] + b_ref[...]
        y_ref[rows, :] = out
        ybf_ref[rows, :] = out.astype(BF16)


def _out_ln(a, w_all, layer, x, g, b, tm, sub):
    m, k = a.shape
    n = w_all.shape[2]
    kern = functools.partial(_out_ln_kernel, sub=sub)
    return pl.pallas_call(
        kern, grid=(m // tm,),
        in_specs=[pl.BlockSpec((tm, k), lambda i: (i, 0)),
                  pl.BlockSpec((None, k, n), lambda i: (layer, 0, 0), pipeline_mode=pl.Buffered(1)),
                  pl.BlockSpec((tm, n), lambda i: (i, 0)),
                  pl.BlockSpec((1, n), lambda i: (0, 0)),
                  pl.BlockSpec((1, n), lambda i: (0, 0))],
        out_specs=[pl.BlockSpec((tm, n), lambda i: (i, 0)),
                   pl.BlockSpec((tm, n), lambda i: (i, 0))],
        out_shape=[jax.ShapeDtypeStruct((m, n), F32), jax.ShapeDtypeStruct((m, n), BF16)],
        compiler_params=_params("parallel"),
        name="out_ln")(a, w_all, x, g, b)


def _group_norm_gate(y, g):
    mu = jnp.mean(y, axis=-1, keepdims=True)
    d = y - mu
    var = jnp.mean(d * d, axis=-1, keepdims=True)
    return _silu(g.astype(F32)) * (d * lax.rsqrt(var + GN_EPS))


def _ret_mixer_kernel(sdp_ref, sds_ref, q_ref, k_ref, v_ref, g_ref, qm_ref, km_ref, vm_ref, gm_ref,
                      ps_ref, st_ref, dm_ref, qd_ref, kd_ref, kdm_ref, dms_ref, qds_ref, kds_ref,
                      *rest, n_chunks, sample_seq):
    o_ref, om_ref, osmp_ref, sp_out_ref, ss_out_ref, s_ref, y_ref = rest[-7:]
    h = pl.program_id(0)
    b = pl.program_id(1)
    part = pl.program_id(2)
    c_len = PROMPT_CHUNK

    @pl.when(part == 0)
    def _():
        qm = qm_ref[...]
        km = km_ref[...]
        vm = vm_ref[...]
        p = (_dot_nt(qm, km) * dm_ref[0, 0:N_META, 0:N_META]).astype(BF16)
        r_meta = pl.multiple_of(b * N_META, N_META)
        om_ref[pl.ds(r_meta, N_META), :] = _group_norm_gate(_dot(p, vm), gm_ref[...]).astype(om_ref.dtype)
        kdk = (km.astype(F32) * kdm_ref[0]).astype(BF16)
        s_ref[...] = _dot_tn(kdk, vm)

    sd = sdp_ref[h]

    def prompt_chunk(c):
        rows = slice(c * c_len, (c + 1) * c_len)
        qc = q_ref[rows, :]
        kc = k_ref[rows, :]
        vc = v_ref[rows, :]
        s = s_ref[...]
        p = (_dot_nt(qc, kc) * dm_ref[0]).astype(BF16)
        y = _dot(p, vc) + _dot(qc, s.astype(BF16)) * qd_ref[0]
        o_ref[rows, :] = _group_norm_gate(y, g_ref[rows, :]).astype(o_ref.dtype)
        kdk = (kc.astype(F32) * kd_ref[0]).astype(BF16)
        s_ref[...] = s * sd + _dot_tn(kdk, vc)

    pf = ps_ref[...].astype(F32)

    def sample_head(gi, hh):
        rows = slice(gi * sample_seq, (gi + 1) * sample_seq)
        qb = pf[rows, hh * RET_DK:(hh + 1) * RET_DK].astype(BF16)
        k = pf[rows, RET_QK + hh * RET_DK:RET_QK + (hh + 1) * RET_DK]
        vb = pf[rows, 2 * RET_QK + hh * RET_DV:2 * RET_QK + (hh + 1) * RET_DV].astype(BF16)
        s = st_ref[gi, hh]
        p = (_dot_nt(qb, k.astype(BF16)) * dms_ref[hh]).astype(BF16)
        y_ref[rows, hh * RET_DV:(hh + 1) * RET_DV] = (
            _dot(p, vb) + _dot(qb, s.astype(BF16)) * qds_ref[hh])
        kdk = (k * kds_ref[hh]).astype(BF16)
        ss_out_ref[gi, hh] = s * sds_ref[hh] + _dot_tn(kdk, vb)

    pairs = [(gi, hh) for gi in range(SAMPLE_GROUP) for hh in range(RET_HEADS)]
    per_chunk = len(pairs) // n_chunks
    for c in range(n_chunks):
        prompt_chunk(c)
        for gi, hh in pairs[c * per_chunk:(c + 1) * per_chunk]:
            sample_head(gi, hh)
    for gi, hh in pairs[n_chunks * per_chunk:]:
        sample_head(gi, hh)
    g0 = 2 * RET_QK + RET_V
    for hh in range(RET_HEADS):
        cols = slice(hh * RET_DV, (hh + 1) * RET_DV)
        g = pf[:, g0 + hh * RET_DV:g0 + (hh + 1) * RET_DV]
        osmp_ref[:, cols] = _group_norm_gate(y_ref[:, cols], g).astype(osmp_ref.dtype)

    @pl.when(part == pl.num_programs(2) - 1)
    def _():
        sp_out_ref[...] = s_ref[...]


def _ret_mixer(pm, ps, state_all, layer, new_p_all, new_s_all, tabs, batch, seq, n_sample, sample_seq):
    c_len = PROMPT_CHUNK
    hd = RET_HEADS
    parts = RET_PARTS
    rows = seq // parts
    grows = SAMPLE_GROUP * sample_seq
    assert hd * batch * parts * SAMPLE_GROUP == n_sample
    n_layers = state_all.shape[0]
    meta0 = n_sample * sample_seq // N_META
    qk_blocks = RET_QK // RET_DK
    v_blocks = 2 * RET_QK // RET_DV
    g_blocks = v_blocks + RET_V // RET_DV

    def grp(h, b, p):
        return (h * batch + b) * parts + p

    kern = functools.partial(_ret_mixer_kernel, n_chunks=rows // c_len, sample_seq=sample_seq)
    p_state_spec = pl.BlockSpec((None, None, None, RET_DK, RET_DV), lambda h, b, p: (layer, b, h, 0, 0))
    s_state_spec = pl.BlockSpec((None, SAMPLE_GROUP, hd, RET_DK, RET_DV),
                                lambda h, b, p: (layer, grp(h, b, p), 0, 0, 0))
    smem = pl.BlockSpec(memory_space=pltpu.SMEM)
    in_specs = [smem, smem,
                pl.BlockSpec((rows, RET_DK), lambda h, b, p: (b * parts + p, h)),
                pl.BlockSpec((rows, RET_DK), lambda h, b, p: (b * parts + p, qk_blocks + h)),
                pl.BlockSpec((rows, RET_DV), lambda h, b, p: (b * parts + p, v_blocks + h)),
                pl.BlockSpec((rows, RET_DV), lambda h, b, p: (b * parts + p, g_blocks + h)),
                pl.BlockSpec((N_META, RET_DK), lambda h, b, p: (meta0 + b, h)),
                pl.BlockSpec((N_META, RET_DK), lambda h, b, p: (meta0 + b, qk_blocks + h)),
                pl.BlockSpec((N_META, RET_DV), lambda h, b, p: (meta0 + b, v_blocks + h)),
                pl.BlockSpec((N_META, RET_DV), lambda h, b, p: (meta0 + b, g_blocks + h)),
                pl.BlockSpec((grows, ps.shape[1]), lambda h, b, p: (grp(h, b, p), 0)),
                s_state_spec,
                pl.BlockSpec((1, c_len, c_len), lambda h, b, p: (h, 0, 0)),
                pl.BlockSpec((1, c_len, 1), lambda h, b, p: (h, 0, 0)),
                pl.BlockSpec((1, c_len, 1), lambda h, b, p: (h, 0, 0)),
                pl.BlockSpec((1, N_META, 1), lambda h, b, p: (h, 0, 0)),
                pl.BlockSpec((hd, sample_seq, sample_seq), lambda h, b, p: (0, 0, 0)),
                pl.BlockSpec((hd, sample_seq, 1), lambda h, b, p: (0, 0, 0)),
                pl.BlockSpec((hd, sample_seq, 1), lambda h, b, p: (0, 0, 0))]
    args = [tabs["sdec_p"], tabs["sdec_s"], pm, pm, pm, pm, ps, ps, ps, ps, ps, state_all,
            tabs["dm_p"], tabs["qd_p"], tabs["kd_p"], tabs["kd_m"], tabs["dm_s"], tabs["qd_s"], tabs["kd_s"]]
    aliases = {}
    if new_p_all is not None:
        in_specs += [pl.BlockSpec(memory_space=pl.ANY), pl.BlockSpec(memory_space=pl.ANY)]
        args += [new_p_all, new_s_all]
        aliases = {len(args) - 2: 3, len(args) - 1: 4}
    return pl.pallas_call(
        kern, grid=(hd, batch, parts),
        in_specs=in_specs,
        out_specs=[pl.BlockSpec((rows, RET_DV), lambda h, b, p: (b * parts + p, h)),
                   pl.BlockSpec((batch * N_META, RET_DV), lambda h, b, p: (0, h)),
                   pl.BlockSpec((grows, RET_V), lambda h, b, p: (grp(h, b, p), 0)),
                   p_state_spec, s_state_spec],
        out_shape=[jax.ShapeDtypeStruct((batch * seq, RET_V), BF16),
                   jax.ShapeDtypeStruct((batch * N_META, RET_V), BF16),
                   jax.ShapeDtypeStruct((n_sample * sample_seq, RET_V), BF16),
                   jax.ShapeDtypeStruct((n_layers, batch, hd, RET_DK, RET_DV), F32),
                   jax.ShapeDtypeStruct(state_all.shape, F32)],
        scratch_shapes=[pltpu.VMEM((RET_DK, RET_DV), F32), pltpu.VMEM((grows, RET_V), F32)],
        input_output_aliases=aliases,
        compiler_params=_params("arbitrary", "arbitrary", "arbitrary"),
        name="ret_mixer")(*args)


def _gated_conv(u, zb, cw_ref, prev1, prev2, t):
    u1 = jnp.where(t >= 1, pltpu.roll(u, 1, 0), prev1)
    u2 = jnp.where(t >= 2, pltpu.roll(u, 2, 0), jnp.where(t == 1, prev1, prev2))
    conv = cw_ref[0:1, :] * u2 + cw_ref[1:2, :] * u1 + cw_ref[2:3, :] * u
    return zb * conv


def _conv_kernel(xm_ref, xs_ref, wb_ref, wc_ref, wv_ref, wz_ref, cw_ref, st_ref,
                 om_ref, os_ref, stp_ref, sts_ref, wbf_ref, meta_tail_ref, tail_ref, *,
                 n_sample, sample_seq, n_prompt, tiles_per_seq):
    t_step = pl.program_id(1)
    tn = om_ref.shape[1]
    n_sample_rows = n_sample * sample_seq

    def project(x):
        u = _dot(x, wbf_ref[1]) * _dot(x, wbf_ref[2])
        zb = _silu(_dot(x, wbf_ref[3])) * _dot(x, wbf_ref[0])
        return u, zb

    @pl.when(t_step == 0)
    def _():
        wbf_ref[0] = wb_ref[...].astype(BF16)
        wbf_ref[1] = wc_ref[...].astype(BF16)
        wbf_ref[2] = wv_ref[...].astype(BF16)
        wbf_ref[3] = wz_ref[...].astype(BF16)
        u, zb = project(xs_ref[...])
        us = u[0:n_sample_rows]
        st = st_ref[...]
        shape3 = (n_sample, sample_seq, tn)
        prev2 = jnp.broadcast_to(st[:, 0:1, :], shape3).reshape(n_sample_rows, tn)
        prev1 = jnp.broadcast_to(st[:, 1:2, :], shape3).reshape(n_sample_rows, tn)
        t = lax.broadcasted_iota(jnp.int32, us.shape, 0) % sample_seq
        os_ref[0:n_sample_rows, :] = _gated_conv(us, zb[0:n_sample_rows], cw_ref, prev1, prev2,
                                                 t).astype(os_ref.dtype)
        sts_ref[...] = us.reshape(shape3)[:, sample_seq - (CONV_WIDTH - 1):sample_seq, :]
        um = u[n_sample_rows:]
        t = lax.broadcasted_iota(jnp.int32, um.shape, 0) % N_META
        zero = jnp.zeros((), F32)
        os_ref[n_sample_rows:, :] = _gated_conv(um, zb[n_sample_rows:], cw_ref, zero, zero,
                                                t).astype(os_ref.dtype)
        for bb in range(n_prompt):
            meta_tail_ref[bb] = um[(bb + 1) * N_META - SUBLANES:(bb + 1) * N_META]

    @pl.when(t_step > 0)
    def _():
        tile = t_step - 1
        bb = tile // tiles_per_seq
        first = tile % tiles_per_seq == 0
        u, zb = project(xm_ref[...])
        rows = u.shape[0]
        tail = jnp.where(first, meta_tail_ref[bb], tail_ref[...])
        prev1 = tail[SUBLANES - 1:SUBLANES, :]
        prev2 = tail[SUBLANES - 2:SUBLANES - 1, :]
        t = lax.broadcasted_iota(jnp.int32, u.shape, 0)
        om_ref[...] = _gated_conv(u, zb, cw_ref, prev1, prev2, t).astype(om_ref.dtype)
        tail_ref[...] = u[rows - SUBLANES:rows]

        @pl.when(tile % tiles_per_seq == tiles_per_seq - 1)
        def _():
            stp_ref[0] = u[rows - (CONV_WIDTH - 1):rows]


def _conv_mixer(xm, xs, w_all, cw_all, layer, state_all, n_prompt, seq, n_sample, sample_seq):
    rm, k = xm.shape
    rs = xs.shape[0]
    e = CONV_E
    tn, tm = CONV_TN, MAIN_TILE
    nt = e // tn
    n_main = rm // tm
    tiles_per_seq = seq // tm
    ns = CONV_WIDTH - 1

    def main_tile(t):
        return jnp.maximum(t - 1, 0)

    w_specs = [pl.BlockSpec((None, k, tn), functools.partial(lambda j, t, q: (layer, 0, q * nt + j), q=q))
               for q in range(4)]
    kern = functools.partial(_conv_kernel, n_sample=n_sample, sample_seq=sample_seq, n_prompt=n_prompt,
                             tiles_per_seq=tiles_per_seq)
    return pl.pallas_call(
        kern, grid=(nt, n_main + 1),
        in_specs=[pl.BlockSpec((tm, k), lambda j, t: (main_tile(t), 0)),
                  pl.BlockSpec((rs, k), lambda j, t: (0, 0))] + w_specs + [
                  pl.BlockSpec((None, CONV_WIDTH, tn), lambda j, t: (layer, 0, j)),
                  pl.BlockSpec((None, n_sample, ns, tn), lambda j, t: (layer, 0, 0, j))],
        out_specs=[pl.BlockSpec((tm, tn), lambda j, t: (main_tile(t), j)),
                   pl.BlockSpec((rs, tn), lambda j, t: (0, j)),
                   pl.BlockSpec((1, ns, tn), lambda j, t: (main_tile(t) // tiles_per_seq, 0, j)),
                   pl.BlockSpec((n_sample, ns, tn), lambda j, t: (0, 0, j))],
        out_shape=[jax.ShapeDtypeStruct((rm, e), BF16),
                   jax.ShapeDtypeStruct((rs, e), BF16),
                   jax.ShapeDtypeStruct((n_prompt, ns, e), F32),
                   jax.ShapeDtypeStruct((n_sample, ns, e), F32)],
        scratch_shapes=[pltpu.VMEM((4, k, tn), BF16),
                        pltpu.VMEM((n_prompt, SUBLANES, tn), F32),
                        pltpu.VMEM((SUBLANES, tn), F32)],
        compiler_params=_params("parallel", "arbitrary"),
        name="conv_mixer")(xm, xs, w_all, w_all, w_all, w_all, cw_all, state_all)


def _rope_tables(pos):
    freqs = ROPE_BASE ** (-jnp.arange(HALF, dtype=F32) * 2.0 / RET_DK)
    ang = pos.astype(F32)[:, None] * freqs[None, :]
    return jnp.cos(ang), jnp.sin(ang)


def _decay_tables(length):
    lg = jnp.log1p(-jnp.exp2(-5.0 - jnp.arange(RET_HEADS, dtype=F32)))
    idx = jnp.arange(length, dtype=F32)
    diff = idx[:, None] - idx[None, :]
    dmask = jnp.exp(jnp.where(diff[None] >= 0, diff[None] * lg[:, None, None], -jnp.inf))
    q_decay = jnp.exp((idx + 1.0)[None, :] * lg[:, None])[:, :, None]
    k_decay = jnp.exp((length - 1.0 - idx)[None, :] * lg[:, None])[:, :, None]
    s_decay = jnp.exp(length * lg)
    return dmask, q_decay, k_decay, s_decay


def kernel(x_prompt, x_sample, state_ret, state_conv, meta, w_ret_in, w_ret_out, w_conv_in, conv_w,
           w_conv_out, ln_g, ln_b):
    bp, lp, d = x_prompt.shape
    bs, ls, _ = x_sample.shape
    n_sample_rows = bs * ls

    xm = x_prompt.reshape(bp * lp, d)
    xs = jnp.concatenate([x_sample.reshape(n_sample_rows, d),
                          jnp.tile(meta.astype(x_prompt.dtype), (bp, 1))], axis=0)
    xm_bf, xs_bf = xm.astype(BF16), xs.astype(BF16)

    rope_m = _rope_tables(N_META + jnp.arange(lp, dtype=jnp.int32))
    pos_s = jnp.concatenate([jnp.tile(PAST_LEN + jnp.arange(ls, dtype=jnp.int32), bs),
                             jnp.tile(jnp.arange(N_META, dtype=jnp.int32), bp)])
    rope_s = _rope_tables(pos_s)

    dm_p, qd_p, kd_p, sdec_p = _decay_tables(PROMPT_CHUNK)
    _, _, kd_m, _ = _decay_tables(N_META)
    dm_s, qd_s, kd_s, sdec_s = _decay_tables(ls)
    tabs = dict(dm_p=dm_p, qd_p=qd_p, kd_p=kd_p, sdec_p=sdec_p, kd_m=kd_m,
                dm_s=dm_s, qd_s=qd_s, kd_s=kd_s, sdec_s=sdec_s)

    w_ret_out_bf = w_ret_out.astype(BF16)
    w_conv_out_bf = w_conv_out.astype(BF16)

    ret_p = ret_s = None
    conv_p, conv_s = [], []
    for i in range(DEPTH):
        j = i // 2
        g_i, b_i = ln_g[i][None, :], ln_b[i][None, :]
        if i % 2 == 0:
            pm, ps = _ret_proj(xm_bf, xs_bf, w_ret_in, j, rope_m, rope_s, lp)
            am, a_meta, a_sample, ret_p, ret_s = _ret_mixer(pm, ps, state_ret, j, ret_p, ret_s, tabs,
                                                            bp, lp, bs, ls)
            a_small = jnp.concatenate([a_sample, a_meta], axis=0)
            w_out = w_ret_out_bf
        else:
            am, a_small, sp, ss = _conv_mixer(xm_bf, xs_bf, w_conv_in, conv_w, j, state_conv, bp, lp, bs, ls)
            conv_p.append(sp)
            conv_s.append(ss)
            w_out = w_conv_out_bf
        xm, xm_bf = _out_ln(am, w_out, j, xm, g_i, b_i, OUT_TM, OUT_SUB)
        xs, xs_bf = _out_ln(a_small, w_out, j, xs, g_i, b_i, xs.shape[0] // 2, xs.shape[0] // 4)

    y_prompt = xm.reshape(bp, lp, d)
    y_sample = xs[:n_sample_rows].reshape(bs, ls, d)
    return (y_prompt, y_sample, ret_p, ret_s, jnp.stack(conv_p), jnp.stack(conv_s))
```

```python
import functools

import jax
import jax.numpy as jnp
from jax import lax
from jax.experimental import pallas as pl
from jax.experimental.pallas import tpu as pltpu

D_MODEL = 2048
DEPTH = 4
PAST_LEN = 16384
N_META = 16
RET_HEADS = 8
RET_DK = D_MODEL // RET_HEADS
RET_DV = 2 * RET_DK
RET_QK = RET_HEADS * RET_DK
RET_V = RET_HEADS * RET_DV
CONV_WIDTH = 3
CONV_E = D_MODEL
ROPE_BASE = 10000.0
LN_EPS = 1e-5
GN_EPS = 1e-6
DEEPNORM_ALPHA = (2.0 * DEPTH) ** 0.25

HALF = RET_DK // 2
SUBLANES = 8
PROMPT_CHUNK = 256
SAMPLE_GROUP = 2
RET_HEAD_GROUP = 2
RET_PARTS = 4
MAIN_TILE = 1024
PROJ_TN = 1024
CONV_TN = 256
PROJ_SUBBLOCKS = 4
OUT_TM = 512
OUT_SUB = 256
VMEM_LIMIT = 56 * 1024 * 1024

BF16 = jnp.bfloat16
F32 = jnp.float32


def _params(*sem):
    return pltpu.CompilerParams(dimension_semantics=sem, vmem_limit_bytes=VMEM_LIMIT)


def _dot(a, b):
    return jnp.dot(a, b, preferred_element_type=F32)


def _dot_nt(a, b):
    return lax.dot_general(a, b, (((1,), (1,)), ((), ())), preferred_element_type=F32)


def _dot_tn(a, b):
    return lax.dot_general(a, b, (((0,), (0,)), ((), ())), preferred_element_type=F32)


def _silu(x):
    return x * (1.0 / (1.0 + jnp.exp(-x)))


def _proj_kernel(xm_ref, xs_ref, w_ref, cm_ref, sm_ref, cs_ref, ss_ref, om_ref, os_ref, wbf_ref, *,
                 n_main, tn):
    j = pl.program_id(0)
    i = pl.program_id(1)
    rope_tiles = 2 * RET_QK // tn
    q_tiles = RET_QK // tn

    @pl.when(i == 0)
    def _():
        wbf_ref[...] = w_ref[...].astype(BF16)

    is_rope = j < rope_tiles
    scale = jnp.where(j < q_tiles, RET_DK ** -0.5, 1.0).astype(F32)

    def path(x_ref, c_ref, s_ref, o_ref):
        sub = x_ref.shape[0] // PROJ_SUBBLOCKS
        for r in range(PROJ_SUBBLOCKS):
            rows = slice(r * sub, (r + 1) * sub)
            acc = _dot(x_ref[rows, :].astype(BF16), wbf_ref[...])
            c = c_ref[rows, :] * scale
            s = s_ref[rows, :] * scale
            for h in range(tn // RET_DK):
                lo = slice(h * RET_DK, h * RET_DK + HALF)
                hi = slice(h * RET_DK + HALF, (h + 1) * RET_DK)
                t1 = acc[:, lo]
                t2 = acc[:, hi]
                o_ref[rows, lo] = jnp.where(is_rope, t1 * c - t2 * s, t1).astype(o_ref.dtype)
                o_ref[rows, hi] = jnp.where(is_rope, t1 * s + t2 * c, t2).astype(o_ref.dtype)

    @pl.when(i < n_main)
    def _():
        path(xm_ref, cm_ref, sm_ref, om_ref)

    @pl.when(i == n_main)
    def _():
        path(xs_ref, cs_ref, ss_ref, os_ref)


def _ret_proj(xm, xs, w_all, layer, rope_m, rope_s, seq):
    rm, k = xm.shape
    rs = xs.shape[0]
    n = w_all.shape[2]
    tm, tn = MAIN_TILE, PROJ_TN
    n_main = rm // tm
    tiles_per_seq = seq // tm
    last = n_main - 1
    kern = functools.partial(_proj_kernel, n_main=n_main, tn=tn)
    return pl.pallas_call(
        kern, grid=(n // tn, n_main + 1),
        in_specs=[pl.BlockSpec((tm, k), lambda j, i: (jnp.minimum(i, last), 0)),
                  pl.BlockSpec((rs, k), lambda j, i: (0, 0)),
                  pl.BlockSpec((None, k, tn), lambda j, i: (layer, 0, j)),
                  pl.BlockSpec((tm, HALF), lambda j, i: (jnp.minimum(i, last) % tiles_per_seq, 0)),
                  pl.BlockSpec((tm, HALF), lambda j, i: (jnp.minimum(i, last) % tiles_per_seq, 0)),
                  pl.BlockSpec((rs, HALF), lambda j, i: (0, 0)),
                  pl.BlockSpec((rs, HALF), lambda j, i: (0, 0))],
        out_specs=[pl.BlockSpec((tm, tn), lambda j, i: (jnp.minimum(i, last), j)),
                   pl.BlockSpec((rs, tn), lambda j, i: (0, j))],
        out_shape=[jax.ShapeDtypeStruct((rm, n), BF16), jax.ShapeDtypeStruct((rs, n), BF16)],
        scratch_shapes=[pltpu.VMEM((k, tn), BF16)],
        compiler_params=_params("parallel", "arbitrary"),
        name="ret_proj")(xm, xs, w_all, rope_m[0], rope_m[1], rope_s[0], rope_s[1])


def _out_ln_kernel(a_ref, w_ref, x_ref, g_ref, b_ref, y_ref, ybf_ref, *, sub):
    for r in range(a_ref.shape[0] // sub):
        rows = slice(r * sub, (r + 1) * sub)
        y = DEEPNORM_ALPHA * x_ref[rows, :] + _dot(a_ref[rows, :], w_ref[...])
        mu = jnp.mean(y, axis=-1, keepdims=True)
        d = y - mu
        var = jnp.mean(d * d, axis=-1, keepdims=True)
        out = d * lax.rsqrt(var + LN_EPS) * g_ref[...] + b_ref[...]
        y_ref[rows, :] = out
        ybf_ref[rows, :] = out.astype(BF16)


def _out_ln(a, w, x, g, b, tm, sub):
    m, k = a.shape
    n = w.shape[1]
    kern = functools.partial(_out_ln_kernel, sub=sub)
    return pl.pallas_call(
        kern, grid=(m // tm,),
        in_specs=[pl.BlockSpec((tm, k), lambda i: (i, 0)),
                  pl.BlockSpec((k, n), lambda i: (0, 0), pipeline_mode=pl.Buffered(1)),
                  pl.BlockSpec((tm, n), lambda i: (i, 0)),
                  pl.BlockSpec((1, n), lambda i: (0, 0)),
                  pl.BlockSpec((1, n), lambda i: (0, 0))],
        out_specs=[pl.BlockSpec((tm, n), lambda i: (i, 0)),
                   pl.BlockSpec((tm, n), lambda i: (i, 0))],
        out_shape=[jax.ShapeDtypeStruct((m, n), F32), jax.ShapeDtypeStruct((m, n), BF16)],
        compiler_params=_params("parallel"),
        name="out_ln")(a, w, x, g, b)


def _group_norm_gate(y, g):
    mu = jnp.mean(y, axis=-1, keepdims=True)
    d = y - mu
    var = jnp.mean(d * d, axis=-1, keepdims=True)
    return _silu(g.astype(F32)) * (d * lax.rsqrt(var + GN_EPS))


def _ret_mixer_kernel(sdp_ref, sds_ref, q_ref, k_ref, v_ref, g_ref, qm_ref, km_ref, vm_ref, gm_ref,
                      ps_ref, st_ref, dm_ref, qd_ref, kd_ref, kdm_ref, dms_ref, qds_ref, kds_ref,
                      *rest, n_chunks, sample_seq):
    o_ref, om_ref, osmp_ref, sp_out_ref, ss_out_ref, s_ref, y_ref = rest[-7:]
    hg = pl.program_id(0)
    b = pl.program_id(1)
    part = pl.program_id(2)
    c_len = PROMPT_CHUNK

    def qk_cols(e):
        return slice(e * RET_DK, (e + 1) * RET_DK)

    def v_cols(e):
        return slice(e * RET_DV, (e + 1) * RET_DV)

    @pl.when(part == 0)
    def _():
        r_meta = pl.multiple_of(b * N_META, N_META)
        for e in range(RET_HEAD_GROUP):
            qm = qm_ref[:, qk_cols(e)]
            km = km_ref[:, qk_cols(e)]
            vm = vm_ref[:, v_cols(e)]
            p = (_dot_nt(qm, km) * dm_ref[e, 0:N_META, 0:N_META]).astype(BF16)
            om_ref[pl.ds(r_meta, N_META), v_cols(e)] = _group_norm_gate(
                _dot(p, vm), gm_ref[:, v_cols(e)]).astype(om_ref.dtype)
            kdk = (km.astype(F32) * kdm_ref[e]).astype(BF16)
            s_ref[e] = _dot_tn(kdk, vm)

    def prompt_chunk(e, c):
        rows = slice(c * c_len, (c + 1) * c_len)
        qc = q_ref[rows, qk_cols(e)]
        kc = k_ref[rows, qk_cols(e)]
        vc = v_ref[rows, v_cols(e)]
        s = s_ref[e]
        p = (_dot_nt(qc, kc) * dm_ref[e]).astype(BF16)
        y = _dot(p, vc) + _dot(qc, s.astype(BF16)) * qd_ref[e]
        o_ref[rows, v_cols(e)] = _group_norm_gate(y, g_ref[rows, v_cols(e)]).astype(o_ref.dtype)
        kdk = (kc.astype(F32) * kd_ref[e]).astype(BF16)
        s_ref[e] = s * sdp_ref[hg * RET_HEAD_GROUP + e] + _dot_tn(kdk, vc)

    pf = ps_ref[...].astype(F32)

    def sample_head(gi, hh):
        rows = slice(gi * sample_seq, (gi + 1) * sample_seq)
        qb = pf[rows, hh * RET_DK:(hh + 1) * RET_DK].astype(BF16)
        k = pf[rows, RET_QK + hh * RET_DK:RET_QK + (hh + 1) * RET_DK]
        vb = pf[rows, 2 * RET_QK + hh * RET_DV:2 * RET_QK + (hh + 1) * RET_DV].astype(BF16)
        s = st_ref[gi, hh]
        p = (_dot_nt(qb, k.astype(BF16)) * dms_ref[hh]).astype(BF16)
        y_ref[rows, hh * RET_DV:(hh + 1) * RET_DV] = (
            _dot(p, vb) + _dot(qb, s.astype(BF16)) * qds_ref[hh])
        kdk = (k * kds_ref[hh]).astype(BF16)
        ss_out_ref[gi, hh] = s * sds_ref[hh] + _dot_tn(kdk, vb)

    pairs = [(gi, hh) for gi in range(SAMPLE_GROUP) for hh in range(RET_HEADS)]
    bodies = [(e, c) for c in range(n_chunks) for e in range(RET_HEAD_GROUP)]
    per_body = len(pairs) // len(bodies)
    for n, (e, c) in enumerate(bodies):
        prompt_chunk(e, c)
        for gi, hh in pairs[n * per_body:(n + 1) * per_body]:
            sample_head(gi, hh)
    for gi, hh in pairs[len(bodies) * per_body:]:
        sample_head(gi, hh)
    g0 = 2 * RET_QK + RET_V
    for hh in range(RET_HEADS):
        cols = slice(hh * RET_DV, (hh + 1) * RET_DV)
        g = pf[:, g0 + hh * RET_DV:g0 + (hh + 1) * RET_DV]
        osmp_ref[:, cols] = _group_norm_gate(y_ref[:, cols], g).astype(osmp_ref.dtype)

    @pl.when(part == pl.num_programs(2) - 1)
    def _():
        sp_out_ref[...] = s_ref[...]


def _ret_mixer(pm, ps, state_all, layer, new_p_all, new_s_all, tabs, batch, seq, n_sample, sample_seq):
    c_len = PROMPT_CHUNK
    hd = RET_HEADS
    hgs = RET_HEAD_GROUP
    n_hg = hd // hgs
    parts = RET_PARTS
    rows = seq // parts
    grows = SAMPLE_GROUP * sample_seq
    assert n_hg * batch * parts * SAMPLE_GROUP == n_sample
    n_layers = state_all.shape[0]
    meta0 = n_sample * sample_seq // N_META
    qkw, vw = hgs * RET_DK, hgs * RET_DV
    k_blocks = RET_QK // qkw
    v_blocks = 2 * RET_QK // vw
    g_blocks = v_blocks + RET_V // vw

    def grp(h, b, p):
        return (h * batch + b) * parts + p

    kern = functools.partial(_ret_mixer_kernel, n_chunks=rows // c_len, sample_seq=sample_seq)
    p_state_spec = pl.BlockSpec((None, None, hgs, RET_DK, RET_DV), lambda h, b, p: (layer, b, h, 0, 0))
    s_state_spec = pl.BlockSpec((None, SAMPLE_GROUP, hd, RET_DK, RET_DV),
                                lambda h, b, p: (layer, grp(h, b, p), 0, 0, 0))
    smem = pl.BlockSpec(memory_space=pltpu.SMEM)
    in_specs = [smem, smem,
                pl.BlockSpec((rows, qkw), lambda h, b, p: (b * parts + p, h)),
                pl.BlockSpec((rows, qkw), lambda h, b, p: (b * parts + p, k_blocks + h)),
                pl.BlockSpec((rows, vw), lambda h, b, p: (b * parts + p, v_blocks + h)),
                pl.BlockSpec((rows, vw), lambda h, b, p: (b * parts + p, g_blocks + h)),
                pl.BlockSpec((N_META, qkw), lambda h, b, p: (meta0 + b, h)),
                pl.BlockSpec((N_META, qkw), lambda h, b, p: (meta0 + b, k_blocks + h)),
                pl.BlockSpec((N_META, vw), lambda h, b, p: (meta0 + b, v_blocks + h)),
                pl.BlockSpec((N_META, vw), lambda h, b, p: (meta0 + b, g_blocks + h)),
                pl.BlockSpec((grows, ps.shape[1]), lambda h, b, p: (grp(h, b, p), 0)),
                s_state_spec,
                pl.BlockSpec((hgs, c_len, c_len), lambda h, b, p: (h, 0, 0)),
                pl.BlockSpec((hgs, c_len, 1), lambda h, b, p: (h, 0, 0)),
                pl.BlockSpec((hgs, c_len, 1), lambda h, b, p: (h, 0, 0)),
                pl.BlockSpec((hgs, N_META, 1), lambda h, b, p: (h, 0, 0)),
                pl.BlockSpec((hd, sample_seq, sample_seq), lambda h, b, p: (0, 0, 0)),
                pl.BlockSpec((hd, sample_seq, 1), lambda h, b, p: (0, 0, 0)),
                pl.BlockSpec((hd, sample_seq, 1), lambda h, b, p: (0, 0, 0))]
    args = [tabs["sdec_p"], tabs["sdec_s"], pm, pm, pm, pm, ps, ps, ps, ps, ps, state_all,
            tabs["dm_p"], tabs["qd_p"], tabs["kd_p"], tabs["kd_m"], tabs["dm_s"], tabs["qd_s"], tabs["kd_s"]]
    aliases = {}
    if new_p_all is not None:
        in_specs += [pl.BlockSpec(memory_space=pl.ANY), pl.BlockSpec(memory_space=pl.ANY)]
        args += [new_p_all, new_s_all]
        aliases = {len(args) - 2: 3, len(args) - 1: 4}
    return pl.pallas_call(
        kern, grid=(n_hg, batch, parts),
        in_specs=in_specs,
        out_specs=[pl.BlockSpec((rows, vw), lambda h, b, p: (b * parts + p, h)),
                   pl.BlockSpec((batch * N_META, vw), lambda h, b, p: (0, h)),
                   pl.BlockSpec((grows, RET_V), lambda h, b, p: (grp(h, b, p), 0)),
                   p_state_spec, s_state_spec],
        out_shape=[jax.ShapeDtypeStruct((batch * seq, RET_V), BF16),
                   jax.ShapeDtypeStruct((batch * N_META, RET_V), BF16),
                   jax.ShapeDtypeStruct((n_sample * sample_seq, RET_V), BF16),
                   jax.ShapeDtypeStruct((n_layers, batch, hd, RET_DK, RET_DV), F32),
                   jax.ShapeDtypeStruct(state_all.shape, F32)],
        scratch_shapes=[pltpu.VMEM((hgs, RET_DK, RET_DV), F32), pltpu.VMEM((grows, RET_V), F32)],
        input_output_aliases=aliases,
        compiler_params=_params("arbitrary", "arbitrary", "arbitrary"),
        name="ret_mixer")(*args)


def _gated_conv(u, zb, cw_ref, prev1, prev2, t):
    u1 = jnp.where(t >= 1, pltpu.roll(u, 1, 0), prev1)
    u2 = jnp.where(t >= 2, pltpu.roll(u, 2, 0), jnp.where(t == 1, prev1, prev2))
    conv = cw_ref[0:1, :] * u2 + cw_ref[1:2, :] * u1 + cw_ref[2:3, :] * u
    return zb * conv


def _conv_kernel(*refs, n_cast, n_sample, sample_seq, n_prompt, tiles_per_seq):
    (xm_ref, xs_ref, wb_ref, wc_ref, wv_ref, wz_ref, cw_ref, st_ref), refs = refs[:8], refs[8:]
    cast_in, refs = refs[:n_cast], refs[n_cast:]
    (om_ref, os_ref, stp_ref, sts_ref), refs = refs[:4], refs[4:]
    cast_out, (wbf_ref, meta_tail_ref, tail_ref) = refs[:n_cast], refs[n_cast:]
    t_step = pl.program_id(1)
    tn = om_ref.shape[1]
    n_sample_rows = n_sample * sample_seq

    for n, (src_ref, dst_ref) in enumerate(zip(cast_in, cast_out)):
        @pl.when(t_step <= n)
        def _(src_ref=src_ref, dst_ref=dst_ref):
            dst_ref[...] = src_ref[...].astype(BF16)

    def project(x):
        u = _dot(x, wbf_ref[1]) * _dot(x, wbf_ref[2])
        zb = _silu(_dot(x, wbf_ref[3])) * _dot(x, wbf_ref[0])
        return u, zb

    @pl.when(t_step == 0)
    def _():
        wbf_ref[0] = wb_ref[...].astype(BF16)
        wbf_ref[1] = wc_ref[...].astype(BF16)
        wbf_ref[2] = wv_ref[...].astype(BF16)
        wbf_ref[3] = wz_ref[...].astype(BF16)
        u, zb = project(xs_ref[...])
        us = u[0:n_sample_rows]
        st = st_ref[...]
        shape3 = (n_sample, sample_seq, tn)
        prev2 = jnp.broadcast_to(st[:, 0:1, :], shape3).reshape(n_sample_rows, tn)
        prev1 = jnp.broadcast_to(st[:, 1:2, :], shape3).reshape(n_sample_rows, tn)
        t = lax.broadcasted_iota(jnp.int32, us.shape, 0) % sample_seq
        os_ref[0:n_sample_rows, :] = _gated_conv(us, zb[0:n_sample_rows], cw_ref, prev1, prev2,
                                                 t).astype(os_ref.dtype)
        sts_ref[...] = us.reshape(shape3)[:, sample_seq - (CONV_WIDTH - 1):sample_seq, :]
        um = u[n_sample_rows:]
        t = lax.broadcasted_iota(jnp.int32, um.shape, 0) % N_META
        zero = jnp.zeros((), F32)
        os_ref[n_sample_rows:, :] = _gated_conv(um, zb[n_sample_rows:], cw_ref, zero, zero,
                                                t).astype(os_ref.dtype)
        for bb in range(n_prompt):
            meta_tail_ref[bb] = um[(bb + 1) * N_META - SUBLANES:(bb + 1) * N_META]

    @pl.when(t_step > 0)
    def _():
        tile = t_step - 1
        bb = tile // tiles_per_seq
        first = tile % tiles_per_seq == 0
        u, zb = project(xm_ref[...])
        rows = u.shape[0]
        tail = jnp.where(first, meta_tail_ref[bb], tail_ref[...])
        prev1 = tail[SUBLANES - 1:SUBLANES, :]
        prev2 = tail[SUBLANES - 2:SUBLANES - 1, :]
        t = lax.broadcasted_iota(jnp.int32, u.shape, 0)
        om_ref[...] = _gated_conv(u, zb, cw_ref, prev1, prev2, t).astype(om_ref.dtype)
        tail_ref[...] = u[rows - SUBLANES:rows]

        @pl.when(tile % tiles_per_seq == tiles_per_seq - 1)
        def _():
            stp_ref[0] = u[rows - (CONV_WIDTH - 1):rows]


def _conv_mixer(xm, xs, w_all, cw_all, layer, state_all, cast_jobs, n_prompt, seq, n_sample, sample_seq):
    rm, k = xm.shape
    rs = xs.shape[0]
    e = CONV_E
    tn, tm = CONV_TN, MAIN_TILE
    nt = e // tn
    n_main = rm // tm
    tiles_per_seq = seq // tm
    ns = CONV_WIDTH - 1

    def main_tile(t):
        return jnp.maximum(t - 1, 0)

    w_specs = [pl.BlockSpec((None, k, tn), functools.partial(lambda j, t, q: (layer, 0, q * nt + j), q=q))
               for q in range(4)]
    cast_in_specs, cast_out_specs, cast_shapes, cast_args = [], [], [], []
    for n, (w_src, src_layer) in enumerate(cast_jobs):
        _, wk, wn = w_src.shape
        rows = wk // (nt * (n + 1))

        def blk(j, t, n=n):
            return j * (n + 1) + jnp.minimum(t, n)

        cast_in_specs.append(pl.BlockSpec((None, rows, wn),
                                          functools.partial(lambda j, t, f, l: (l, f(j, t), 0), f=blk, l=src_layer)))
        cast_out_specs.append(pl.BlockSpec((rows, wn), functools.partial(lambda j, t, f: (f(j, t), 0), f=blk)))
        cast_shapes.append(jax.ShapeDtypeStruct((wk, wn), BF16))
        cast_args.append(w_src)
    kern = functools.partial(_conv_kernel, n_cast=len(cast_jobs), n_sample=n_sample, sample_seq=sample_seq,
                             n_prompt=n_prompt, tiles_per_seq=tiles_per_seq)
    return pl.pallas_call(
        kern, grid=(nt, n_main + 1),
        in_specs=[pl.BlockSpec((tm, k), lambda j, t: (main_tile(t), 0)),
                  pl.BlockSpec((rs, k), lambda j, t: (0, 0))] + w_specs + [
                  pl.BlockSpec((None, CONV_WIDTH, tn), lambda j, t: (layer, 0, j)),
                  pl.BlockSpec((None, n_sample, ns, tn), lambda j, t: (layer, 0, 0, j))] + cast_in_specs,
        out_specs=[pl.BlockSpec((tm, tn), lambda j, t: (main_tile(t), j)),
                   pl.BlockSpec((rs, tn), lambda j, t: (0, j)),
                   pl.BlockSpec((1, ns, tn), lambda j, t: (main_tile(t) // tiles_per_seq, 0, j)),
                   pl.BlockSpec((n_sample, ns, tn), lambda j, t: (0, 0, j))] + cast_out_specs,
        out_shape=[jax.ShapeDtypeStruct((rm, e), BF16),
                   jax.ShapeDtypeStruct((rs, e), BF16),
                   jax.ShapeDtypeStruct((n_prompt, ns, e), F32),
                   jax.ShapeDtypeStruct((n_sample, ns, e), F32)] + cast_shapes,
        scratch_shapes=[pltpu.VMEM((4, k, tn), BF16),
                        pltpu.VMEM((n_prompt, SUBLANES, tn), F32),
                        pltpu.VMEM((SUBLANES, tn), F32)],
        compiler_params=_params("arbitrary", "arbitrary"),
        name="conv_mixer")(xm, xs, w_all, w_all, w_all, w_all, cw_all, state_all, *cast_args)


def _rope_tables(pos):
    freqs = ROPE_BASE ** (-jnp.arange(HALF, dtype=F32) * 2.0 / RET_DK)
    ang = pos.astype(F32)[:, None] * freqs[None, :]
    return jnp.cos(ang), jnp.sin(ang)


def _decay_tables(length):
    lg = jnp.log1p(-jnp.exp2(-5.0 - jnp.arange(RET_HEADS, dtype=F32)))
    idx = jnp.arange(length, dtype=F32)
    diff = idx[:, None] - idx[None, :]
    dmask = jnp.exp(jnp.where(diff[None] >= 0, diff[None] * lg[:, None, None], -jnp.inf))
    q_decay = jnp.exp((idx + 1.0)[None, :] * lg[:, None])[:, :, None]
    k_decay = jnp.exp((length - 1.0 - idx)[None, :] * lg[:, None])[:, :, None]
    s_decay = jnp.exp(length * lg)
    return dmask, q_decay, k_decay, s_decay


def kernel(x_prompt, x_sample, state_ret, state_conv, meta, w_ret_in, w_ret_out, w_conv_in, conv_w,
           w_conv_out, ln_g, ln_b):
    bp, lp, d = x_prompt.shape
    bs, ls, _ = x_sample.shape
    n_sample_rows = bs * ls

    xm = x_prompt.reshape(bp * lp, d)
    xs = jnp.concatenate([x_sample.reshape(n_sample_rows, d),
                          jnp.tile(meta.astype(x_prompt.dtype), (bp, 1))], axis=0)
    xm_bf, xs_bf = xm, xs.astype(BF16)

    rope_m = _rope_tables(N_META + jnp.arange(lp, dtype=jnp.int32))
    pos_s = jnp.concatenate([jnp.tile(PAST_LEN + jnp.arange(ls, dtype=jnp.int32), bs),
                             jnp.tile(jnp.arange(N_META, dtype=jnp.int32), bp)])
    rope_s = _rope_tables(pos_s)

    dm_p, qd_p, kd_p, sdec_p = _decay_tables(PROMPT_CHUNK)
    _, _, kd_m, _ = _decay_tables(N_META)
    dm_s, qd_s, kd_s, sdec_s = _decay_tables(ls)
    tabs = dict(dm_p=dm_p, qd_p=qd_p, kd_p=kd_p, sdec_p=sdec_p, kd_m=kd_m,
                dm_s=dm_s, qd_s=qd_s, kd_s=kd_s, sdec_s=sdec_s)

    w_ret_out_next = w_ret_out[0].astype(BF16)

    ret_p = ret_s = None
    conv_p, conv_s = [], []
    for i in range(DEPTH):
        j = i // 2
        g_i, b_i = ln_g[i][None, :], ln_b[i][None, :]
        if i % 2 == 0:
            pm, ps = _ret_proj(xm_bf, xs_bf, w_ret_in, j, rope_m, rope_s, lp)
            am, a_meta, a_sample, ret_p, ret_s = _ret_mixer(pm, ps, state_ret, j, ret_p, ret_s, tabs,
                                                            bp, lp, bs, ls)
            a_small = jnp.concatenate([a_sample, a_meta], axis=0)
            w_out = w_ret_out_next
        else:
            jobs = [(w_conv_out, j)]
            if j + 1 < w_ret_out.shape[0]:
                jobs.append((w_ret_out, j + 1))
            am, a_small, sp, ss, w_out, *nxt = _conv_mixer(xm_bf, xs_bf, w_conv_in, conv_w, j, state_conv,
                                                           jobs, bp, lp, bs, ls)
            if nxt:
                w_ret_out_next = nxt[0]
            conv_p.append(sp)
            conv_s.append(ss)
        xm, xm_bf = _out_ln(am, w_out, xm, g_i, b_i, OUT_TM, OUT_SUB)
        xs, xs_bf = _out_ln(a_small, w_out, xs, g_i, b_i, xs.shape[0] // 2, xs.shape[0] // 4)

    y_prompt = xm.reshape(bp, lp, d)
    y_sample = xs[:n_sample_rows].reshape(bs, ls, d)
    return (y_prompt, y_sample, ret_p, ret_s, jnp.stack(conv_p), jnp.stack(conv_s))
```

```python
import functools

import jax
import jax.numpy as jnp
from jax import lax
from jax.experimental import pallas as pl
from jax.experimental.pallas import tpu as pltpu

D_MODEL = 2048
DEPTH = 4
PAST_LEN = 16384
N_META = 16
RET_HEADS = 8
RET_DK = D_MODEL // RET_HEADS
RET_DV = 2 * RET_DK
RET_QK = RET_HEADS * RET_DK
RET_V = RET_HEADS * RET_DV
CONV_WIDTH = 3
CONV_E = D_MODEL
ROPE_BASE = 10000.0
LN_EPS = 1e-5
GN_EPS = 1e-6
DEEPNORM_ALPHA = (2.0 * DEPTH) ** 0.25

HALF = RET_DK // 2
SUBLANES = 8
PROMPT_CHUNK = 256
SAMPLE_GROUP = 2
RET_HEAD_GROUP = 2
RET_PARTS = 4
MAIN_TILE = 1024
PROJ_TN = 1024
CONV_TN = 256
PROJ_SUBBLOCKS = 4
OUT_TM = 512
OUT_SUB = 256
STATE_SLOTS = 3
VMEM_LIMIT = 56 * 1024 * 1024
VMEM_LIMIT_RING = 60 * 1024 * 1024

BF16 = jnp.bfloat16
F32 = jnp.float32


def _params(*sem):
    return pltpu.CompilerParams(dimension_semantics=sem, vmem_limit_bytes=VMEM_LIMIT)


def _dot(a, b):
    return jnp.dot(a, b, preferred_element_type=F32)


def _dot_nt(a, b):
    return lax.dot_general(a, b, (((1,), (1,)), ((), ())), preferred_element_type=F32)


def _dot_tn(a, b):
    return lax.dot_general(a, b, (((0,), (0,)), ((), ())), preferred_element_type=F32)


def _silu(x):
    return x * (1.0 / (1.0 + jnp.exp(-x)))


def _proj_kernel(xm_ref, xs_ref, w_ref, cm_ref, sm_ref, cs_ref, ss_ref, om_ref, os_ref, wbf_ref, *,
                 n_main, tn):
    j = pl.program_id(0)
    i = pl.program_id(1)
    rope_tiles = 2 * RET_QK // tn
    q_tiles = RET_QK // tn

    @pl.when(i == 0)
    def _():
        wbf_ref[...] = w_ref[...].astype(BF16)

    is_rope = j < rope_tiles
    scale = jnp.where(j < q_tiles, RET_DK ** -0.5, 1.0).astype(F32)

    def path(x_ref, c_ref, s_ref, o_ref):
        sub = x_ref.shape[0] // PROJ_SUBBLOCKS
        for r in range(PROJ_SUBBLOCKS):
            rows = slice(r * sub, (r + 1) * sub)
            acc = _dot(x_ref[rows, :].astype(BF16), wbf_ref[...])
            c = c_ref[rows, :] * scale
            s = s_ref[rows, :] * scale
            for h in range(tn // RET_DK):
                lo = slice(h * RET_DK, h * RET_DK + HALF)
                hi = slice(h * RET_DK + HALF, (h + 1) * RET_DK)
                t1 = acc[:, lo]
                t2 = acc[:, hi]
                o_ref[rows, lo] = jnp.where(is_rope, t1 * c - t2 * s, t1).astype(o_ref.dtype)
                o_ref[rows, hi] = jnp.where(is_rope, t1 * s + t2 * c, t2).astype(o_ref.dtype)

    @pl.when(i < n_main)
    def _():
        path(xm_ref, cm_ref, sm_ref, om_ref)

    @pl.when(i == n_main)
    def _():
        path(xs_ref, cs_ref, ss_ref, os_ref)


def _ret_proj(xm, xs, w_all, layer, rope_m, rope_s, seq):
    rm, k = xm.shape
    rs = xs.shape[0]
    n = w_all.shape[2]
    tm, tn = MAIN_TILE, PROJ_TN
    n_main = rm // tm
    tiles_per_seq = seq // tm
    last = n_main - 1
    kern = functools.partial(_proj_kernel, n_main=n_main, tn=tn)
    return pl.pallas_call(
        kern, grid=(n // tn, n_main + 1),
        in_specs=[pl.BlockSpec((tm, k), lambda j, i: (jnp.minimum(i, last), 0)),
                  pl.BlockSpec((rs, k), lambda j, i: (0, 0)),
                  pl.BlockSpec((None, k, tn), lambda j, i: (layer, 0, j)),
                  pl.BlockSpec((tm, HALF), lambda j, i: (jnp.minimum(i, last) % tiles_per_seq, 0)),
                  pl.BlockSpec((tm, HALF), lambda j, i: (jnp.minimum(i, last) % tiles_per_seq, 0)),
                  pl.BlockSpec((rs, HALF), lambda j, i: (0, 0)),
                  pl.BlockSpec((rs, HALF), lambda j, i: (0, 0))],
        out_specs=[pl.BlockSpec((tm, tn), lambda j, i: (jnp.minimum(i, last), j)),
                   pl.BlockSpec((rs, tn), lambda j, i: (0, j))],
        out_shape=[jax.ShapeDtypeStruct((rm, n), BF16), jax.ShapeDtypeStruct((rs, n), BF16)],
        scratch_shapes=[pltpu.VMEM((k, tn), BF16)],
        compiler_params=_params("parallel", "arbitrary"),
        name="ret_proj")(xm, xs, w_all, rope_m[0], rope_m[1], rope_s[0], rope_s[1])


def _out_ln_kernel(a_ref, w_ref, x_ref, g_ref, b_ref, y_ref, ybf_ref, *, sub):
    for r in range(a_ref.shape[0] // sub):
        rows = slice(r * sub, (r + 1) * sub)
        y = DEEPNORM_ALPHA * x_ref[rows, :] + _dot(a_ref[rows, :], w_ref[...])
        mu = jnp.mean(y, axis=-1, keepdims=True)
        d = y - mu
        var = jnp.mean(d * d, axis=-1, keepdims=True)
        out = d * lax.rsqrt(var + LN_EPS) * g_ref[...] + b_ref[...]
        y_ref[rows, :] = out
        ybf_ref[rows, :] = out.astype(BF16)


def _out_ln(a, w, x, g, b, tm, sub):
    m, k = a.shape
    n = w.shape[1]
    kern = functools.partial(_out_ln_kernel, sub=sub)
    return pl.pallas_call(
        kern, grid=(m // tm,),
        in_specs=[pl.BlockSpec((tm, k), lambda i: (i, 0)),
                  pl.BlockSpec((k, n), lambda i: (0, 0), pipeline_mode=pl.Buffered(1)),
                  pl.BlockSpec((tm, n), lambda i: (i, 0)),
                  pl.BlockSpec((1, n), lambda i: (0, 0)),
                  pl.BlockSpec((1, n), lambda i: (0, 0))],
        out_specs=[pl.BlockSpec((tm, n), lambda i: (i, 0)),
                   pl.BlockSpec((tm, n), lambda i: (i, 0))],
        out_shape=[jax.ShapeDtypeStruct((m, n), F32), jax.ShapeDtypeStruct((m, n), BF16)],
        compiler_params=_params("parallel"),
        name="out_ln")(a, w, x, g, b)


def _group_norm_gate(y, g):
    mu = jnp.mean(y, axis=-1, keepdims=True)
    d = y - mu
    var = jnp.mean(d * d, axis=-1, keepdims=True)
    return _silu(g.astype(F32)) * (d * lax.rsqrt(var + GN_EPS))


def _ret_mixer_kernel(sdp_ref, sds_ref, q_ref, k_ref, v_ref, g_ref, qm_ref, km_ref, vm_ref, gm_ref,
                      ps_ref, st_hbm, dm_ref, qd_ref, kd_ref, kdm_ref, dms_ref, qds_ref, kds_ref,
                      *rest, n_chunks, sample_seq, layer):
    o_ref, om_ref, osmp_ref, sp_out_ref, ss_out_ref, s_ref, y_ref, st_buf, st_sem = rest[-9:]
    hg = pl.program_id(0)
    b = pl.program_id(1)
    part = pl.program_id(2)
    c_len = PROMPT_CHUNK

    step = (hg * pl.num_programs(1) + b) * pl.num_programs(2) + part
    n_steps = pl.num_programs(0) * pl.num_programs(1) * pl.num_programs(2)

    def state_copy(s_idx):
        slot = s_idx % STATE_SLOTS
        src = st_hbm.at[layer, pl.ds(s_idx * SAMPLE_GROUP, SAMPLE_GROUP)]
        return pltpu.make_async_copy(src, st_buf.at[slot], st_sem.at[slot])

    @pl.when(step == 0)
    def _():
        for s0 in range(STATE_SLOTS - 1):
            state_copy(s0).start()

    @pl.when(step + (STATE_SLOTS - 1) < n_steps)
    def _():
        state_copy(step + (STATE_SLOTS - 1)).start()

    state_copy(step).wait()
    st_ref = st_buf.at[step % STATE_SLOTS]

    def qk_cols(e):
        return slice(e * RET_DK, (e + 1) * RET_DK)

    def v_cols(e):
        return slice(e * RET_DV, (e + 1) * RET_DV)

    @pl.when(part == 0)
    def _():
        r_meta = pl.multiple_of(b * N_META, N_META)
        for e in range(RET_HEAD_GROUP):
            qm = qm_ref[:, qk_cols(e)]
            km = km_ref[:, qk_cols(e)]
            vm = vm_ref[:, v_cols(e)]
            p = (_dot_nt(qm, km) * dm_ref[e, 0:N_META, 0:N_META]).astype(BF16)
            om_ref[pl.ds(r_meta, N_META), v_cols(e)] = _group_norm_gate(
                _dot(p, vm), gm_ref[:, v_cols(e)]).astype(om_ref.dtype)
            kdk = (km.astype(F32) * kdm_ref[e]).astype(BF16)
            s_ref[e] = _dot_tn(kdk, vm)

    def prompt_chunk(e, c):
        rows = slice(c * c_len, (c + 1) * c_len)
        qc = q_ref[rows, qk_cols(e)]
        kc = k_ref[rows, qk_cols(e)]
        vc = v_ref[rows, v_cols(e)]
        s = s_ref[e]
        p = (_dot_nt(qc, kc) * dm_ref[e]).astype(BF16)
        y = _dot(p, vc) + _dot(qc, s.astype(BF16)) * qd_ref[e]
        o_ref[rows, v_cols(e)] = _group_norm_gate(y, g_ref[rows, v_cols(e)]).astype(o_ref.dtype)
        kdk = (kc.astype(F32) * kd_ref[e]).astype(BF16)
        s_ref[e] = s * sdp_ref[hg * RET_HEAD_GROUP + e] + _dot_tn(kdk, vc)

    pf = ps_ref[...].astype(F32)

    def sample_head(gi, hh):
        rows = slice(gi * sample_seq, (gi + 1) * sample_seq)
        qb = pf[rows, hh * RET_DK:(hh + 1) * RET_DK].astype(BF16)
        k = pf[rows, RET_QK + hh * RET_DK:RET_QK + (hh + 1) * RET_DK]
        vb = pf[rows, 2 * RET_QK + hh * RET_DV:2 * RET_QK + (hh + 1) * RET_DV].astype(BF16)
        s = st_ref[gi, hh]
        p = (_dot_nt(qb, k.astype(BF16)) * dms_ref[hh]).astype(BF16)
        y_ref[rows, hh * RET_DV:(hh + 1) * RET_DV] = (
            _dot(p, vb) + _dot(qb, s.astype(BF16)) * qds_ref[hh])
        kdk = (k * kds_ref[hh]).astype(BF16)
        ss_out_ref[gi, hh] = s * sds_ref[hh] + _dot_tn(kdk, vb)

    pairs = [(gi, hh) for gi in range(SAMPLE_GROUP) for hh in range(RET_HEADS)]
    bodies = [(e, c) for c in range(n_chunks) for e in range(RET_HEAD_GROUP)]
    per_body = len(pairs) // len(bodies)
    for n, (e, c) in enumerate(bodies):
        prompt_chunk(e, c)
        for gi, hh in pairs[n * per_body:(n + 1) * per_body]:
            sample_head(gi, hh)
    for gi, hh in pairs[len(bodies) * per_body:]:
        sample_head(gi, hh)
    g0 = 2 * RET_QK + RET_V
    for hh in range(RET_HEADS):
        cols = slice(hh * RET_DV, (hh + 1) * RET_DV)
        g = pf[:, g0 + hh * RET_DV:g0 + (hh + 1) * RET_DV]
        osmp_ref[:, cols] = _group_norm_gate(y_ref[:, cols], g).astype(osmp_ref.dtype)

    @pl.when(part == pl.num_programs(2) - 1)
    def _():
        sp_out_ref[...] = s_ref[...]


def _ret_mixer(pm, ps, state_all, layer, new_p_all, new_s_all, tabs, batch, seq, n_sample, sample_seq):
    c_len = PROMPT_CHUNK
    hd = RET_HEADS
    hgs = RET_HEAD_GROUP
    n_hg = hd // hgs
    parts = RET_PARTS
    rows = seq // parts
    grows = SAMPLE_GROUP * sample_seq
    assert n_hg * batch * parts * SAMPLE_GROUP == n_sample
    n_layers = state_all.shape[0]
    meta0 = n_sample * sample_seq // N_META
    qkw, vw = hgs * RET_DK, hgs * RET_DV
    k_blocks = RET_QK // qkw
    v_blocks = 2 * RET_QK // vw
    g_blocks = v_blocks + RET_V // vw

    def grp(h, b, p):
        return (h * batch + b) * parts + p

    kern = functools.partial(_ret_mixer_kernel, n_chunks=rows // c_len, sample_seq=sample_seq, layer=layer)
    p_state_spec = pl.BlockSpec((None, None, hgs, RET_DK, RET_DV), lambda h, b, p: (layer, b, h, 0, 0))
    s_state_spec = pl.BlockSpec((None, SAMPLE_GROUP, hd, RET_DK, RET_DV),
                                lambda h, b, p: (layer, grp(h, b, p), 0, 0, 0))
    smem = pl.BlockSpec(memory_space=pltpu.SMEM)
    in_specs = [smem, smem,
                pl.BlockSpec((rows, qkw), lambda h, b, p: (b * parts + p, h)),
                pl.BlockSpec((rows, qkw), lambda h, b, p: (b * parts + p, k_blocks + h)),
                pl.BlockSpec((rows, vw), lambda h, b, p: (b * parts + p, v_blocks + h)),
                pl.BlockSpec((rows, vw), lambda h, b, p: (b * parts + p, g_blocks + h)),
                pl.BlockSpec((N_META, qkw), lambda h, b, p: (meta0 + b, h)),
                pl.BlockSpec((N_META, qkw), lambda h, b, p: (meta0 + b, k_blocks + h)),
                pl.BlockSpec((N_META, vw), lambda h, b, p: (meta0 + b, v_blocks + h)),
                pl.BlockSpec((N_META, vw), lambda h, b, p: (meta0 + b, g_blocks + h)),
                pl.BlockSpec((grows, ps.shape[1]), lambda h, b, p: (grp(h, b, p), 0)),
                pl.BlockSpec(memory_space=pl.ANY),
                pl.BlockSpec((hgs, c_len, c_len), lambda h, b, p: (h, 0, 0)),
                pl.BlockSpec((hgs, c_len, 1), lambda h, b, p: (h, 0, 0)),
                pl.BlockSpec((hgs, c_len, 1), lambda h, b, p: (h, 0, 0)),
                pl.BlockSpec((hgs, N_META, 1), lambda h, b, p: (h, 0, 0)),
                pl.BlockSpec((hd, sample_seq, sample_seq), lambda h, b, p: (0, 0, 0)),
                pl.BlockSpec((hd, sample_seq, 1), lambda h, b, p: (0, 0, 0)),
                pl.BlockSpec((hd, sample_seq, 1), lambda h, b, p: (0, 0, 0))]
    args = [tabs["sdec_p"], tabs["sdec_s"], pm, pm, pm, pm, ps, ps, ps, ps, ps, state_all,
            tabs["dm_p"], tabs["qd_p"], tabs["kd_p"], tabs["kd_m"], tabs["dm_s"], tabs["qd_s"], tabs["kd_s"]]
    aliases = {}
    if new_p_all is not None:
        in_specs += [pl.BlockSpec(memory_space=pl.ANY), pl.BlockSpec(memory_space=pl.ANY)]
        args += [new_p_all, new_s_all]
        aliases = {len(args) - 2: 3, len(args) - 1: 4}
    return pl.pallas_call(
        kern, grid=(n_hg, batch, parts),
        in_specs=in_specs,
        out_specs=[pl.BlockSpec((rows, vw), lambda h, b, p: (b * parts + p, h)),
                   pl.BlockSpec((batch * N_META, vw), lambda h, b, p: (0, h)),
                   pl.BlockSpec((grows, RET_V), lambda h, b, p: (grp(h, b, p), 0)),
                   p_state_spec, s_state_spec],
        out_shape=[jax.ShapeDtypeStruct((batch * seq, RET_V), BF16),
                   jax.ShapeDtypeStruct((batch * N_META, RET_V), BF16),
                   jax.ShapeDtypeStruct((n_sample * sample_seq, RET_V), BF16),
                   jax.ShapeDtypeStruct((n_layers, batch, hd, RET_DK, RET_DV), F32),
                   jax.ShapeDtypeStruct(state_all.shape, F32)],
        scratch_shapes=[pltpu.VMEM((hgs, RET_DK, RET_DV), F32), pltpu.VMEM((grows, RET_V), F32),
                        pltpu.VMEM((STATE_SLOTS, SAMPLE_GROUP, hd, RET_DK, RET_DV), F32),
                        pltpu.SemaphoreType.DMA((STATE_SLOTS,))],
        input_output_aliases=aliases,
        compiler_params=pltpu.CompilerParams(dimension_semantics=("arbitrary",) * 3,
                                             vmem_limit_bytes=VMEM_LIMIT_RING),
        name="ret_mixer")(*args)


def _gated_conv(u, zb, cw_ref, prev1, prev2, t):
    u1 = jnp.where(t >= 1, pltpu.roll(u, 1, 0), prev1)
    u2 = jnp.where(t >= 2, pltpu.roll(u, 2, 0), jnp.where(t == 1, prev1, prev2))
    conv = cw_ref[0:1, :] * u2 + cw_ref[1:2, :] * u1 + cw_ref[2:3, :] * u
    return zb * conv


def _conv_kernel(*refs, n_cast, n_sample, sample_seq, n_prompt, tiles_per_seq):
    (xm_ref, xs_ref, wb_ref, wc_ref, wv_ref, wz_ref, cw_ref, st_ref), refs = refs[:8], refs[8:]
    cast_in, refs = refs[:n_cast], refs[n_cast:]
    (om_ref, os_ref, stp_ref, sts_ref), refs = refs[:4], refs[4:]
    cast_out, (wbf_ref, meta_tail_ref, tail_ref) = refs[:n_cast], refs[n_cast:]
    t_step = pl.program_id(1)
    tn = om_ref.shape[1]
    n_sample_rows = n_sample * sample_seq

    for n, (src_ref, dst_ref) in enumerate(zip(cast_in, cast_out)):
        @pl.when(t_step <= n)
        def _(src_ref=src_ref, dst_ref=dst_ref):
            dst_ref[...] = src_ref[...].astype(BF16)

    def project(x):
        u = _dot(x, wbf_ref[1]) * _dot(x, wbf_ref[2])
        zb = _silu(_dot(x, wbf_ref[3])) * _dot(x, wbf_ref[0])
        return u, zb

    @pl.when(t_step == 0)
    def _():
        wbf_ref[0] = wb_ref[...].astype(BF16)
        wbf_ref[1] = wc_ref[...].astype(BF16)
        wbf_ref[2] = wv_ref[...].astype(BF16)
        wbf_ref[3] = wz_ref[...].astype(BF16)
        u, zb = project(xs_ref[...])
        us = u[0:n_sample_rows]
        st = st_ref[...]
        shape3 = (n_sample, sample_seq, tn)
        prev2 = jnp.broadcast_to(st[:, 0:1, :], shape3).reshape(n_sample_rows, tn)
        prev1 = jnp.broadcast_to(st[:, 1:2, :], shape3).reshape(n_sample_rows, tn)
        t = lax.broadcasted_iota(jnp.int32, us.shape, 0) % sample_seq
        os_ref[0:n_sample_rows, :] = _gated_conv(us, zb[0:n_sample_rows], cw_ref, prev1, prev2,
                                                 t).astype(os_ref.dtype)
        sts_ref[...] = us.reshape(shape3)[:, sample_seq - (CONV_WIDTH - 1):sample_seq, :]
        um = u[n_sample_rows:]
        t = lax.broadcasted_iota(jnp.int32, um.shape, 0) % N_META
        zero = jnp.zeros((), F32)
        os_ref[n_sample_rows:, :] = _gated_conv(um, zb[n_sample_rows:], cw_ref, zero, zero,
                                                t).astype(os_ref.dtype)
        for bb in range(n_prompt):
            meta_tail_ref[bb] = um[(bb + 1) * N_META - SUBLANES:(bb + 1) * N_META]

    @pl.when(t_step > 0)
    def _():
        tile = t_step - 1
        bb = tile // tiles_per_seq
        first = tile % tiles_per_seq == 0
        u, zb = project(xm_ref[...])
        rows = u.shape[0]
        tail = jnp.where(first, meta_tail_ref[bb], tail_ref[...])
        prev1 = tail[SUBLANES - 1:SUBLANES, :]
        prev2 = tail[SUBLANES - 2:SUBLANES - 1, :]
        t = lax.broadcasted_iota(jnp.int32, u.shape, 0)
        om_ref[...] = _gated_conv(u, zb, cw_ref, prev1, prev2, t).astype(om_ref.dtype)
        tail_ref[...] = u[rows - SUBLANES:rows]

        @pl.when(tile % tiles_per_seq == tiles_per_seq - 1)
        def _():
            stp_ref[0] = u[rows - (CONV_WIDTH - 1):rows]


def _conv_mixer(xm, xs, w_all, cw_all, layer, state_all, cast_jobs, n_prompt, seq, n_sample, sample_seq):
    rm, k = xm.shape
    rs = xs.shape[0]
    e = CONV_E
    tn, tm = CONV_TN, MAIN_TILE
    nt = e // tn
    n_main = rm // tm
    tiles_per_seq = seq // tm
    ns = CONV_WIDTH - 1

    def main_tile(t):
        return jnp.maximum(t - 1, 0)

    w_specs = [pl.BlockSpec((None, k, tn), functools.partial(lambda j, t, q: (layer, 0, q * nt + j), q=q))
               for q in range(4)]
    cast_in_specs, cast_out_specs, cast_shapes, cast_args = [], [], [], []
    for n, (w_src, src_layer) in enumerate(cast_jobs):
        _, wk, wn = w_src.shape
        rows = wk // (nt * (n + 1))

        def blk(j, t, n=n):
            return j * (n + 1) + jnp.minimum(t, n)

        cast_in_specs.append(pl.BlockSpec((None, rows, wn),
                                          functools.partial(lambda j, t, f, l: (l, f(j, t), 0), f=blk, l=src_layer)))
        cast_out_specs.append(pl.BlockSpec((rows, wn), functools.partial(lambda j, t, f: (f(j, t), 0), f=blk)))
        cast_shapes.append(jax.ShapeDtypeStruct((wk, wn), BF16))
        cast_args.append(w_src)
    kern = functools.partial(_conv_kernel, n_cast=len(cast_jobs), n_sample=n_sample, sample_seq=sample_seq,
                             n_prompt=n_prompt, tiles_per_seq=tiles_per_seq)
    return pl.pallas_call(
        kern, grid=(nt, n_main + 1),
        in_specs=[pl.BlockSpec((tm, k), lambda j, t: (main_tile(t), 0)),
                  pl.BlockSpec((rs, k), lambda j, t: (0, 0))] + w_specs + [
                  pl.BlockSpec((None, CONV_WIDTH, tn), lambda j, t: (layer, 0, j)),
                  pl.BlockSpec((None, n_sample, ns, tn), lambda j, t: (layer, 0, 0, j))] + cast_in_specs,
        out_specs=[pl.BlockSpec((tm, tn), lambda j, t: (main_tile(t), j)),
                   pl.BlockSpec((rs, tn), lambda j, t: (0, j)),
                   pl.BlockSpec((1, ns, tn), lambda j, t: (main_tile(t) // tiles_per_seq, 0, j)),
                   pl.BlockSpec((n_sample, ns, tn), lambda j, t: (0, 0, j))] + cast_out_specs,
        out_shape=[jax.ShapeDtypeStruct((rm, e), BF16),
                   jax.ShapeDtypeStruct((rs, e), BF16),
                   jax.ShapeDtypeStruct((n_prompt, ns, e), F32),
                   jax.ShapeDtypeStruct((n_sample, ns, e), F32)] + cast_shapes,
        scratch_shapes=[pltpu.VMEM((4, k, tn), BF16),
                        pltpu.VMEM((n_prompt, SUBLANES, tn), F32),
                        pltpu.VMEM((SUBLANES, tn), F32)],
        compiler_params=_params("arbitrary", "arbitrary"),
        name="conv_mixer")(xm, xs, w_all, w_all, w_all, w_all, cw_all, state_all, *cast_args)


def _rope_tables(pos):
    freqs = ROPE_BASE ** (-jnp.arange(HALF, dtype=F32) * 2.0 / RET_DK)
    ang = pos.astype(F32)[:, None] * freqs[None, :]
    return jnp.cos(ang), jnp.sin(ang)


def _decay_tables(length):
    lg = jnp.log1p(-jnp.exp2(-5.0 - jnp.arange(RET_HEADS, dtype=F32)))
    idx = jnp.arange(length, dtype=F32)
    diff = idx[:, None] - idx[None, :]
    dmask = jnp.exp(jnp.where(diff[None] >= 0, diff[None] * lg[:, None, None], -jnp.inf))
    q_decay = jnp.exp((idx + 1.0)[None, :] * lg[:, None])[:, :, None]
    k_decay = jnp.exp((length - 1.0 - idx)[None, :] * lg[:, None])[:, :, None]
    s_decay = jnp.exp(length * lg)
    return dmask, q_decay, k_decay, s_decay


def kernel(x_prompt, x_sample, state_ret, state_conv, meta, w_ret_in, w_ret_out, w_conv_in, conv_w,
           w_conv_out, ln_g, ln_b):
    bp, lp, d = x_prompt.shape
    bs, ls, _ = x_sample.shape
    n_sample_rows = bs * ls

    xm = x_prompt.reshape(bp * lp, d)
    xs = jnp.concatenate([x_sample.reshape(n_sample_rows, d),
                          jnp.tile(meta.astype(x_prompt.dtype), (bp, 1))], axis=0)
    xm_bf, xs_bf = xm, xs.astype(BF16)

    rope_m = _rope_tables(N_META + jnp.arange(lp, dtype=jnp.int32))
    pos_s = jnp.concatenate([jnp.tile(PAST_LEN + jnp.arange(ls, dtype=jnp.int32), bs),
                             jnp.tile(jnp.arange(N_META, dtype=jnp.int32), bp)])
    rope_s = _rope_tables(pos_s)

    dm_p, qd_p, kd_p, sdec_p = _decay_tables(PROMPT_CHUNK)
    _, _, kd_m, _ = _decay_tables(N_META)
    dm_s, qd_s, kd_s, sdec_s = _decay_tables(ls)
    tabs = dict(dm_p=dm_p, qd_p=qd_p, kd_p=kd_p, sdec_p=sdec_p, kd_m=kd_m,
                dm_s=dm_s, qd_s=qd_s, kd_s=kd_s, sdec_s=sdec_s)

    w_ret_out_next = w_ret_out[0].astype(BF16)

    ret_p = ret_s = None
    conv_p, conv_s = [], []
    for i in range(DEPTH):
        j = i // 2
        g_i, b_i = ln_g[i][None, :], ln_b[i][None, :]
        if i % 2 == 0:
            pm, ps = _ret_proj(xm_bf, xs_bf, w_ret_in, j, rope_m, rope_s, lp)
            am, a_meta, a_sample, ret_p, ret_s = _ret_mixer(pm, ps, state_ret, j, ret_p, ret_s, tabs,
                                                            bp, lp, bs, ls)
            a_small = jnp.concatenate([a_sample, a_meta], axis=0)
            w_out = w_ret_out_next
        else:
            jobs = [(w_conv_out, j)]
            if j + 1 < w_ret_out.shape[0]:
                jobs.append((w_ret_out, j + 1))
            am, a_small, sp, ss, w_out, *nxt = _conv_mixer(xm_bf, xs_bf, w_conv_in, conv_w, j, state_conv,
                                                           jobs, bp, lp, bs, ls)
            if nxt:
                w_ret_out_next = nxt[0]
            conv_p.append(sp)
            conv_s.append(ss)
        xm, xm_bf = _out_ln(am, w_out, xm, g_i, b_i, OUT_TM, OUT_SUB)
        xs, xs_bf = _out_ln(a_small, w_out, xs, g_i, b_i, xs.shape[0] // 2, xs.shape[0] // 4)

    y_prompt = xm.reshape(bp, lp, d)
    y_sample = xs[:n_sample_rows].reshape(bs, ls, d)
    return (y_prompt, y_sample, ret_p, ret_s, jnp.stack(conv_p), jnp.stack(conv_s))
```

```python
import functools

import jax
import jax.numpy as jnp
from jax import lax
from jax.experimental import pallas as pl
from jax.experimental.pallas import tpu as pltpu

D_MODEL = 2048
DEPTH = 4
PAST_LEN = 16384
N_META = 16
RET_HEADS = 8
RET_DK = D_MODEL // RET_HEADS
RET_DV = 2 * RET_DK
RET_QK = RET_HEADS * RET_DK
RET_V = RET_HEADS * RET_DV
CONV_WIDTH = 3
CONV_E = D_MODEL
ROPE_BASE = 10000.0
LN_EPS = 1e-5
GN_EPS = 1e-6
DEEPNORM_ALPHA = (2.0 * DEPTH) ** 0.25

HALF = RET_DK // 2
SUBLANES = 8
PROMPT_CHUNK = 256
SAMPLE_GROUP = 2
RET_HEAD_GROUP = 2
RET_PARTS = 4
MAIN_TILE = 1024
PROJ_TN = 1024
CONV_TN = 256
PROJ_SUBBLOCKS = 4
PROJ_SUB_ROWS = 256
CAST_ROWS = 256
OUT_TM = 512
OUT_SUB = 256
STATE_SLOTS = 3
VMEM_LIMIT = 56 * 1024 * 1024
VMEM_LIMIT_HIGH = 60 * 1024 * 1024

BF16 = jnp.bfloat16
F32 = jnp.float32


def _params(*sem):
    return pltpu.CompilerParams(dimension_semantics=sem, vmem_limit_bytes=VMEM_LIMIT)


def _dot(a, b):
    return jnp.dot(a, b, preferred_element_type=F32)


def _dot_nt(a, b):
    return lax.dot_general(a, b, (((1,), (1,)), ((), ())), preferred_element_type=F32)


def _dot_tn(a, b):
    return lax.dot_general(a, b, (((0,), (0,)), ((), ())), preferred_element_type=F32)


def _silu(x):
    return x * (1.0 / (1.0 + jnp.exp(-x)))


def _proj_kernel(*refs, n_main, tn, cast_steps):
    (xm_ref, xs_ref, w_ref, cm_ref, sm_ref, cs_ref, ss_ref), refs = refs[:7], refs[7:]
    n_cast = 1 if cast_steps else 0
    cast_in, (om_ref, os_ref), cast_out, wbf_ref = (refs[:n_cast], refs[n_cast:n_cast + 2],
                                                    refs[n_cast + 2:2 * n_cast + 2], refs[-1])
    j = pl.program_id(0)
    i = pl.program_id(1)
    rope_tiles = 2 * RET_QK // tn
    q_tiles = RET_QK // tn

    @pl.when(i == 0)
    def _():
        wbf_ref[...] = w_ref[...].astype(BF16)

    if cast_steps:
        cast_j, cast_i = cast_steps

        @pl.when((j < cast_j) & (i < cast_i))
        def _():
            cast_out[0][...] = cast_in[0][...].astype(BF16)

    is_rope = j < rope_tiles
    scale = jnp.where(j < q_tiles, RET_DK ** -0.5, 1.0).astype(F32)

    def path(x_ref, c_ref, s_ref, o_ref):
        n_sub = max(PROJ_SUBBLOCKS, x_ref.shape[0] // PROJ_SUB_ROWS)
        sub = x_ref.shape[0] // n_sub
        for r in range(n_sub):
            rows = slice(r * sub, (r + 1) * sub)
            acc = _dot(x_ref[rows, :].astype(BF16), wbf_ref[...])
            c = c_ref[rows, :] * scale
            s = s_ref[rows, :] * scale
            for h in range(tn // RET_DK):
                lo = slice(h * RET_DK, h * RET_DK + HALF)
                hi = slice(h * RET_DK + HALF, (h + 1) * RET_DK)
                t1 = acc[:, lo]
                t2 = acc[:, hi]
                o_ref[rows, lo] = jnp.where(is_rope, t1 * c - t2 * s, t1).astype(o_ref.dtype)
                o_ref[rows, hi] = jnp.where(is_rope, t1 * s + t2 * c, t2).astype(o_ref.dtype)

    @pl.when(i < n_main)
    def _():
        path(xm_ref, cm_ref, sm_ref, om_ref)

    @pl.when(i == n_main)
    def _():
        path(xs_ref, cs_ref, ss_ref, os_ref)


def _ret_proj(xm, xs, w_all, layer, rope_m, rope_s, seq, cast_job=None):
    rm, k = xm.shape
    rs = xs.shape[0]
    n = w_all.shape[2]
    tn = PROJ_TN
    tm = MAIN_TILE if xm.dtype == F32 else 2 * MAIN_TILE
    n_main = rm // tm
    tiles_per_seq = seq // tm
    last = n_main - 1
    in_specs = [pl.BlockSpec((tm, k), lambda j, i: (jnp.minimum(i, last), 0)),
                pl.BlockSpec((rs, k), lambda j, i: (0, 0)),
                pl.BlockSpec((None, k, tn), lambda j, i: (layer, 0, j)),
                pl.BlockSpec((tm, HALF), lambda j, i: (jnp.minimum(i, last) % tiles_per_seq, 0)),
                pl.BlockSpec((tm, HALF), lambda j, i: (jnp.minimum(i, last) % tiles_per_seq, 0)),
                pl.BlockSpec((rs, HALF), lambda j, i: (0, 0)),
                pl.BlockSpec((rs, HALF), lambda j, i: (0, 0))]
    out_specs = [pl.BlockSpec((tm, tn), lambda j, i: (jnp.minimum(i, last), j)),
                 pl.BlockSpec((rs, tn), lambda j, i: (0, j))]
    out_shape = [jax.ShapeDtypeStruct((rm, n), BF16), jax.ShapeDtypeStruct((rs, n), BF16)]
    args = [xm, xs, w_all, rope_m[0], rope_m[1], rope_s[0], rope_s[1]]
    cast_steps = None
    if cast_job is not None:
        w_src, src_layer = cast_job
        _, wk, wn = w_src.shape
        cast_i = 2
        cast_j = wk // (CAST_ROWS * cast_i)
        assert cast_j <= n // tn and cast_i <= n_main
        cast_steps = (cast_j, cast_i)
        n_blk = wk // CAST_ROWS

        def blk(j, i):
            return jnp.minimum(j * cast_i + jnp.minimum(i, cast_i - 1), n_blk - 1)

        in_specs.append(pl.BlockSpec((None, CAST_ROWS, wn), lambda j, i: (src_layer, blk(j, i), 0)))
        out_specs.append(pl.BlockSpec((CAST_ROWS, wn), lambda j, i: (blk(j, i), 0)))
        out_shape.append(jax.ShapeDtypeStruct((wk, wn), BF16))
        args.append(w_src)
    kern = functools.partial(_proj_kernel, n_main=n_main, tn=tn, cast_steps=cast_steps)
    return pl.pallas_call(
        kern, grid=(n // tn, n_main + 1),
        in_specs=in_specs, out_specs=out_specs, out_shape=out_shape,
        scratch_shapes=[pltpu.VMEM((k, tn), BF16)],
        compiler_params=pltpu.CompilerParams(dimension_semantics=("arbitrary", "arbitrary"),
                                             vmem_limit_bytes=VMEM_LIMIT_HIGH),
        name="ret_proj")(*args)


def _out_ln_kernel(a_ref, w_ref, x_ref, g_ref, b_ref, y_ref, ybf_ref, *, sub):
    for r in range(a_ref.shape[0] // sub):
        rows = slice(r * sub, (r + 1) * sub)
        y = DEEPNORM_ALPHA * x_ref[rows, :] + _dot(a_ref[rows, :], w_ref[...])
        mu = jnp.mean(y, axis=-1, keepdims=True)
        d = y - mu
        var = jnp.mean(d * d, axis=-1, keepdims=True)
        out = d * lax.rsqrt(var + LN_EPS) * g_ref[...] + b_ref[...]
        y_ref[rows, :] = out
        ybf_ref[rows, :] = out.astype(BF16)


def _out_ln(a, w, x, g, b, tm, sub, rows=None):
    m, k = a.shape
    m = m if rows is None else rows
    n = w.shape[1]
    kern = functools.partial(_out_ln_kernel, sub=sub)
    return pl.pallas_call(
        kern, grid=(m // tm,),
        in_specs=[pl.BlockSpec((tm, k), lambda i: (i, 0)),
                  pl.BlockSpec((k, n), lambda i: (0, 0), pipeline_mode=pl.Buffered(1)),
                  pl.BlockSpec((tm, n), lambda i: (i, 0)),
                  pl.BlockSpec((1, n), lambda i: (0, 0)),
                  pl.BlockSpec((1, n), lambda i: (0, 0))],
        out_specs=[pl.BlockSpec((tm, n), lambda i: (i, 0)),
                   pl.BlockSpec((tm, n), lambda i: (i, 0))],
        out_shape=[jax.ShapeDtypeStruct((m, n), F32), jax.ShapeDtypeStruct((m, n), BF16)],
        compiler_params=_params("parallel"),
        name="out_ln")(a, w, x, g, b)


def _group_norm_gate(y, g):
    mu = jnp.mean(y, axis=-1, keepdims=True)
    d = y - mu
    var = jnp.mean(d * d, axis=-1, keepdims=True)
    return _silu(g.astype(F32)) * (d * lax.rsqrt(var + GN_EPS))


def _ret_mixer_kernel(sdp_ref, sds_ref, q_ref, k_ref, v_ref, g_ref, qm_ref, km_ref, vm_ref, gm_ref,
                      ps_ref, st_hbm, dm_ref, qd_ref, kd_ref, kdm_ref, dms_ref, qds_ref, kds_ref,
                      *rest, n_chunks, sample_seq, layer):
    o_ref, om_ref, osmp_ref, sp_out_ref, ss_out_ref, s_ref, y_ref, st_buf, st_sem = rest[-9:]
    hg = pl.program_id(0)
    b = pl.program_id(1)
    part = pl.program_id(2)
    c_len = PROMPT_CHUNK

    step = (hg * pl.num_programs(1) + b) * pl.num_programs(2) + part
    n_steps = pl.num_programs(0) * pl.num_programs(1) * pl.num_programs(2)

    def state_copy(s_idx):
        slot = s_idx % STATE_SLOTS
        src = st_hbm.at[layer, pl.ds(s_idx * SAMPLE_GROUP, SAMPLE_GROUP)]
        return pltpu.make_async_copy(src, st_buf.at[slot], st_sem.at[slot])

    @pl.when(step == 0)
    def _():
        for s0 in range(STATE_SLOTS - 1):
            state_copy(s0).start()

    @pl.when(step + (STATE_SLOTS - 1) < n_steps)
    def _():
        state_copy(step + (STATE_SLOTS - 1)).start()

    state_copy(step).wait()
    st_ref = st_buf.at[step % STATE_SLOTS]

    def qk_cols(e):
        return slice(e * RET_DK, (e + 1) * RET_DK)

    def v_cols(e):
        return slice(e * RET_DV, (e + 1) * RET_DV)

    @pl.when(part == 0)
    def _():
        r_meta = pl.multiple_of(b * N_META, N_META)
        for e in range(RET_HEAD_GROUP):
            qm = qm_ref[:, qk_cols(e)]
            km = km_ref[:, qk_cols(e)]
            vm = vm_ref[:, v_cols(e)]
            p = (_dot_nt(qm, km) * dm_ref[e, 0:N_META, 0:N_META]).astype(BF16)
            om_ref[pl.ds(r_meta, N_META), v_cols(e)] = _group_norm_gate(
                _dot(p, vm), gm_ref[:, v_cols(e)]).astype(om_ref.dtype)
            kdk = (km.astype(F32) * kdm_ref[e]).astype(BF16)
            s_ref[e] = _dot_tn(kdk, vm)

    def prompt_chunk(e, c):
        rows = slice(c * c_len, (c + 1) * c_len)
        qc = q_ref[rows, qk_cols(e)]
        kc = k_ref[rows, qk_cols(e)]
        vc = v_ref[rows, v_cols(e)]
        s = s_ref[e]
        p = (_dot_nt(qc, kc) * dm_ref[e]).astype(BF16)
        y = _dot(p, vc) + _dot(qc, s.astype(BF16)) * qd_ref[e]
        o_ref[rows, v_cols(e)] = _group_norm_gate(y, g_ref[rows, v_cols(e)]).astype(o_ref.dtype)
        kdk = (kc.astype(F32) * kd_ref[e]).astype(BF16)
        s_ref[e] = s * sdp_ref[hg * RET_HEAD_GROUP + e] + _dot_tn(kdk, vc)

    pf = ps_ref[...].astype(F32)

    def sample_head(gi, hh):
        rows = slice(gi * sample_seq, (gi + 1) * sample_seq)
        qb = pf[rows, hh * RET_DK:(hh + 1) * RET_DK].astype(BF16)
        k = pf[rows, RET_QK + hh * RET_DK:RET_QK + (hh + 1) * RET_DK]
        vb = pf[rows, 2 * RET_QK + hh * RET_DV:2 * RET_QK + (hh + 1) * RET_DV].astype(BF16)
        s = st_ref[gi, hh]
        p = (_dot_nt(qb, k.astype(BF16)) * dms_ref[hh]).astype(BF16)
        y_ref[rows, hh * RET_DV:(hh + 1) * RET_DV] = (
            _dot(p, vb) + _dot(qb, s.astype(BF16)) * qds_ref[hh])
        kdk = (k * kds_ref[hh]).astype(BF16)
        ss_out_ref[gi, hh] = s * sds_ref[hh] + _dot_tn(kdk, vb)

    pairs = [(gi, hh) for gi in range(SAMPLE_GROUP) for hh in range(RET_HEADS)]
    bodies = [(e, c) for c in range(n_chunks) for e in range(RET_HEAD_GROUP)]
    per_body = len(pairs) // len(bodies)
    for n, (e, c) in enumerate(bodies):
        prompt_chunk(e, c)
        for gi, hh in pairs[n * per_body:(n + 1) * per_body]:
            sample_head(gi, hh)
    for gi, hh in pairs[len(bodies) * per_body:]:
        sample_head(gi, hh)
    g0 = 2 * RET_QK + RET_V
    for hh in range(RET_HEADS):
        cols = slice(hh * RET_DV, (hh + 1) * RET_DV)
        g = pf[:, g0 + hh * RET_DV:g0 + (hh + 1) * RET_DV]
        osmp_ref[:, cols] = _group_norm_gate(y_ref[:, cols], g).astype(osmp_ref.dtype)

    @pl.when(part == pl.num_programs(2) - 1)
    def _():
        sp_out_ref[...] = s_ref[...]


def _ret_mixer(pm, ps, state_all, layer, new_p_all, new_s_all, tabs, batch, seq, n_sample, sample_seq):
    c_len = PROMPT_CHUNK
    hd = RET_HEADS
    hgs = RET_HEAD_GROUP
    n_hg = hd // hgs
    parts = RET_PARTS
    rows = seq // parts
    grows = SAMPLE_GROUP * sample_seq
    assert n_hg * batch * parts * SAMPLE_GROUP == n_sample
    n_layers = state_all.shape[0]
    meta0 = n_sample * sample_seq // N_META
    qkw, vw = hgs * RET_DK, hgs * RET_DV
    k_blocks = RET_QK // qkw
    v_blocks = 2 * RET_QK // vw
    g_blocks = v_blocks + RET_V // vw

    def grp(h, b, p):
        return (h * batch + b) * parts + p

    kern = functools.partial(_ret_mixer_kernel, n_chunks=rows // c_len, sample_seq=sample_seq, layer=layer)
    p_state_spec = pl.BlockSpec((None, None, hgs, RET_DK, RET_DV), lambda h, b, p: (layer, b, h, 0, 0))
    s_state_spec = pl.BlockSpec((None, SAMPLE_GROUP, hd, RET_DK, RET_DV),
                                lambda h, b, p: (layer, grp(h, b, p), 0, 0, 0))
    smem = pl.BlockSpec(memory_space=pltpu.SMEM)
    in_specs = [smem, smem,
                pl.BlockSpec((rows, qkw), lambda h, b, p: (b * parts + p, h)),
                pl.BlockSpec((rows, qkw), lambda h, b, p: (b * parts + p, k_blocks + h)),
                pl.BlockSpec((rows, vw), lambda h, b, p: (b * parts + p, v_blocks + h)),
                pl.BlockSpec((rows, vw), lambda h, b, p: (b * parts + p, g_blocks + h)),
                pl.BlockSpec((N_META, qkw), lambda h, b, p: (meta0 + b, h)),
                pl.BlockSpec((N_META, qkw), lambda h, b, p: (meta0 + b, k_blocks + h)),
                pl.BlockSpec((N_META, vw), lambda h, b, p: (meta0 + b, v_blocks + h)),
                pl.BlockSpec((N_META, vw), lambda h, b, p: (meta0 + b, g_blocks + h)),
                pl.BlockSpec((grows, ps.shape[1]), lambda h, b, p: (grp(h, b, p), 0)),
                pl.BlockSpec(memory_space=pl.ANY),
                pl.BlockSpec((hgs, c_len, c_len), lambda h, b, p: (h, 0, 0)),
                pl.BlockSpec((hgs, c_len, 1), lambda h, b, p: (h, 0, 0)),
                pl.BlockSpec((hgs, c_len, 1), lambda h, b, p: (h, 0, 0)),
                pl.BlockSpec((hgs, N_META, 1), lambda h, b, p: (h, 0, 0)),
                pl.BlockSpec((hd, sample_seq, sample_seq), lambda h, b, p: (0, 0, 0)),
                pl.BlockSpec((hd, sample_seq, 1), lambda h, b, p: (0, 0, 0)),
                pl.BlockSpec((hd, sample_seq, 1), lambda h, b, p: (0, 0, 0))]
    args = [tabs["sdec_p"], tabs["sdec_s"], pm, pm, pm, pm, ps, ps, ps, ps, ps, state_all,
            tabs["dm_p"], tabs["qd_p"], tabs["kd_p"], tabs["kd_m"], tabs["dm_s"], tabs["qd_s"], tabs["kd_s"]]
    aliases = {}
    if new_p_all is not None:
        in_specs += [pl.BlockSpec(memory_space=pl.ANY), pl.BlockSpec(memory_space=pl.ANY)]
        args += [new_p_all, new_s_all]
        aliases = {len(args) - 2: 3, len(args) - 1: 4}
    return pl.pallas_call(
        kern, grid=(n_hg, batch, parts),
        in_specs=in_specs,
        out_specs=[pl.BlockSpec((rows, vw), lambda h, b, p: (b * parts + p, h)),
                   pl.BlockSpec((batch * N_META, vw), lambda h, b, p: (0, h)),
                   pl.BlockSpec((grows, RET_V), lambda h, b, p: (grp(h, b, p), 0)),
                   p_state_spec, s_state_spec],
        out_shape=[jax.ShapeDtypeStruct((batch * seq, RET_V), BF16),
                   jax.ShapeDtypeStruct((batch * N_META, RET_V), BF16),
                   jax.ShapeDtypeStruct((n_sample * sample_seq, RET_V), BF16),
                   jax.ShapeDtypeStruct((n_layers, batch, hd, RET_DK, RET_DV), F32),
                   jax.ShapeDtypeStruct(state_all.shape, F32)],
        scratch_shapes=[pltpu.VMEM((hgs, RET_DK, RET_DV), F32), pltpu.VMEM((grows, RET_V), F32),
                        pltpu.VMEM((STATE_SLOTS, SAMPLE_GROUP, hd, RET_DK, RET_DV), F32),
                        pltpu.SemaphoreType.DMA((STATE_SLOTS,))],
        input_output_aliases=aliases,
        compiler_params=pltpu.CompilerParams(dimension_semantics=("arbitrary",) * 3,
                                             vmem_limit_bytes=VMEM_LIMIT_HIGH),
        name="ret_mixer")(*args)


def _gated_conv(u, zb, cw_ref, prev1, prev2, t):
    u1 = jnp.where(t >= 1, pltpu.roll(u, 1, 0), prev1)
    u2 = jnp.where(t >= 2, pltpu.roll(u, 2, 0), jnp.where(t == 1, prev1, prev2))
    conv = cw_ref[0:1, :] * u2 + cw_ref[1:2, :] * u1 + cw_ref[2:3, :] * u
    return zb * conv


def _conv_kernel(*refs, n_cast, n_sample, sample_seq, n_prompt, tiles_per_seq):
    (xm_ref, xs_ref, wb_ref, wc_ref, wv_ref, wz_ref, cw_ref, st_ref), refs = refs[:8], refs[8:]
    cast_in, refs = refs[:n_cast], refs[n_cast:]
    (om_ref, os_ref, stp_ref, sts_ref), refs = refs[:4], refs[4:]
    cast_out, (wbf_ref, meta_tail_ref, tail_ref) = refs[:n_cast], refs[n_cast:]
    t_step = pl.program_id(1)
    tn = om_ref.shape[1]
    n_sample_rows = n_sample * sample_seq

    for n, (src_ref, dst_ref) in enumerate(zip(cast_in, cast_out)):
        @pl.when(t_step <= n)
        def _(src_ref=src_ref, dst_ref=dst_ref):
            dst_ref[...] = src_ref[...].astype(BF16)

    def project(x):
        u = _dot(x, wbf_ref[1]) * _dot(x, wbf_ref[2])
        zb = _silu(_dot(x, wbf_ref[3])) * _dot(x, wbf_ref[0])
        return u, zb

    @pl.when(t_step == 0)
    def _():
        wbf_ref[0] = wb_ref[...].astype(BF16)
        wbf_ref[1] = wc_ref[...].astype(BF16)
        wbf_ref[2] = wv_ref[...].astype(BF16)
        wbf_ref[3] = wz_ref[...].astype(BF16)
        u, zb = project(xs_ref[...])
        us = u[0:n_sample_rows]
        st = st_ref[...]
        shape3 = (n_sample, sample_seq, tn)
        prev2 = jnp.broadcast_to(st[:, 0:1, :], shape3).reshape(n_sample_rows, tn)
        prev1 = jnp.broadcast_to(st[:, 1:2, :], shape3).reshape(n_sample_rows, tn)
        t = lax.broadcasted_iota(jnp.int32, us.shape, 0) % sample_seq
        os_ref[0:n_sample_rows, :] = _gated_conv(us, zb[0:n_sample_rows], cw_ref, prev1, prev2,
                                                 t).astype(os_ref.dtype)
        sts_ref[...] = us.reshape(shape3)[:, sample_seq - (CONV_WIDTH - 1):sample_seq, :]
        um = u[n_sample_rows:]
        t = lax.broadcasted_iota(jnp.int32, um.shape, 0) % N_META
        zero = jnp.zeros((), F32)
        os_ref[n_sample_rows:, :] = _gated_conv(um, zb[n_sample_rows:], cw_ref, zero, zero,
                                                t).astype(os_ref.dtype)
        for bb in range(n_prompt):
            meta_tail_ref[bb] = um[(bb + 1) * N_META - SUBLANES:(bb + 1) * N_META]

    @pl.when(t_step > 0)
    def _():
        tile = t_step - 1
        bb = tile // tiles_per_seq
        first = tile % tiles_per_seq == 0
        u, zb = project(xm_ref[...])
        rows = u.shape[0]
        tail = jnp.where(first, meta_tail_ref[bb], tail_ref[...])
        prev1 = tail[SUBLANES - 1:SUBLANES, :]
        prev2 = tail[SUBLANES - 2:SUBLANES - 1, :]
        t = lax.broadcasted_iota(jnp.int32, u.shape, 0)
        om_ref[...] = _gated_conv(u, zb, cw_ref, prev1, prev2, t).astype(om_ref.dtype)
        tail_ref[...] = u[rows - SUBLANES:rows]

        @pl.when(tile % tiles_per_seq == tiles_per_seq - 1)
        def _():
            stp_ref[0] = u[rows - (CONV_WIDTH - 1):rows]


def _conv_mixer(xm, xs, w_all, cw_all, layer, state_all, cast_jobs, n_prompt, seq, n_sample, sample_seq):
    rm, k = xm.shape
    rs = xs.shape[0]
    e = CONV_E
    tn, tm = CONV_TN, MAIN_TILE
    nt = e // tn
    n_main = rm // tm
    tiles_per_seq = seq // tm
    ns = CONV_WIDTH - 1

    def main_tile(t):
        return jnp.maximum(t - 1, 0)

    w_specs = [pl.BlockSpec((None, k, tn), functools.partial(lambda j, t, q: (layer, 0, q * nt + j), q=q))
               for q in range(4)]
    cast_in_specs, cast_out_specs, cast_shapes, cast_args = [], [], [], []
    for n, (w_src, src_layer) in enumerate(cast_jobs):
        _, wk, wn = w_src.shape
        rows = wk // (nt * (n + 1))

        def blk(j, t, n=n):
            return j * (n + 1) + jnp.minimum(t, n)

        cast_in_specs.append(pl.BlockSpec((None, rows, wn),
                                          functools.partial(lambda j, t, f, l: (l, f(j, t), 0), f=blk, l=src_layer)))
        cast_out_specs.append(pl.BlockSpec((rows, wn), functools.partial(lambda j, t, f: (f(j, t), 0), f=blk)))
        cast_shapes.append(jax.ShapeDtypeStruct((wk, wn), BF16))
        cast_args.append(w_src)
    kern = functools.partial(_conv_kernel, n_cast=len(cast_jobs), n_sample=n_sample, sample_seq=sample_seq,
                             n_prompt=n_prompt, tiles_per_seq=tiles_per_seq)
    return pl.pallas_call(
        kern, grid=(nt, n_main + 1),
        in_specs=[pl.BlockSpec((tm, k), lambda j, t: (main_tile(t), 0)),
                  pl.BlockSpec((rs, k), lambda j, t: (0, 0))] + w_specs + [
                  pl.BlockSpec((None, CONV_WIDTH, tn), lambda j, t: (layer, 0, j)),
                  pl.BlockSpec((None, n_sample, ns, tn), lambda j, t: (layer, 0, 0, j))] + cast_in_specs,
        out_specs=[pl.BlockSpec((tm, tn), lambda j, t: (main_tile(t), j)),
                   pl.BlockSpec((rs, tn), lambda j, t: (0, j)),
                   pl.BlockSpec((1, ns, tn), lambda j, t: (main_tile(t) // tiles_per_seq, 0, j)),
                   pl.BlockSpec((n_sample, ns, tn), lambda j, t: (0, 0, j))] + cast_out_specs,
        out_shape=[jax.ShapeDtypeStruct((rm, e), BF16),
                   jax.ShapeDtypeStruct((rs, e), BF16),
                   jax.ShapeDtypeStruct((n_prompt, ns, e), F32),
                   jax.ShapeDtypeStruct((n_sample, ns, e), F32)] + cast_shapes,
        scratch_shapes=[pltpu.VMEM((4, k, tn), BF16),
                        pltpu.VMEM((n_prompt, SUBLANES, tn), F32),
                        pltpu.VMEM((SUBLANES, tn), F32)],
        compiler_params=_params("arbitrary", "arbitrary"),
        name="conv_mixer")(xm, xs, w_all, w_all, w_all, w_all, cw_all, state_all, *cast_args)


def _rope_tables(pos):
    freqs = ROPE_BASE ** (-jnp.arange(HALF, dtype=F32) * 2.0 / RET_DK)
    ang = pos.astype(F32)[:, None] * freqs[None, :]
    return jnp.cos(ang), jnp.sin(ang)


def _decay_tables(length):
    lg = jnp.log1p(-jnp.exp2(-5.0 - jnp.arange(RET_HEADS, dtype=F32)))
    idx = jnp.arange(length, dtype=F32)
    diff = idx[:, None] - idx[None, :]
    dmask = jnp.exp(jnp.where(diff[None] >= 0, diff[None] * lg[:, None, None], -jnp.inf))
    q_decay = jnp.exp((idx + 1.0)[None, :] * lg[:, None])[:, :, None]
    k_decay = jnp.exp((length - 1.0 - idx)[None, :] * lg[:, None])[:, :, None]
    s_decay = jnp.exp(length * lg)
    return dmask, q_decay, k_decay, s_decay


def kernel(x_prompt, x_sample, state_ret, state_conv, meta, w_ret_in, w_ret_out, w_conv_in, conv_w,
           w_conv_out, ln_g, ln_b):
    bp, lp, d = x_prompt.shape
    bs, ls, _ = x_sample.shape
    n_sample_rows = bs * ls

    xm = x_prompt.reshape(bp * lp, d)
    xs = jnp.concatenate([x_sample.reshape(n_sample_rows, d),
                          jnp.tile(meta.astype(x_prompt.dtype), (bp, 1))], axis=0)
    xm_bf, xs_bf = xm, xs.astype(BF16)

    rope_m = _rope_tables(N_META + jnp.arange(lp, dtype=jnp.int32))
    pos_s = jnp.concatenate([jnp.tile(PAST_LEN + jnp.arange(ls, dtype=jnp.int32), bs),
                             jnp.tile(jnp.arange(N_META, dtype=jnp.int32), bp)])
    rope_s = _rope_tables(pos_s)

    dm_p, qd_p, kd_p, sdec_p = _decay_tables(PROMPT_CHUNK)
    _, _, kd_m, _ = _decay_tables(N_META)
    dm_s, qd_s, kd_s, sdec_s = _decay_tables(ls)
    tabs = dict(dm_p=dm_p, qd_p=qd_p, kd_p=kd_p, sdec_p=sdec_p, kd_m=kd_m,
                dm_s=dm_s, qd_s=qd_s, kd_s=kd_s, sdec_s=sdec_s)

    w_ret_out_next = None

    ret_p = ret_s = None
    conv_p, conv_s = [], []
    for i in range(DEPTH):
        j = i // 2
        g_i, b_i = ln_g[i][None, :], ln_b[i][None, :]
        if i % 2 == 0:
            job = (w_ret_out, j) if w_ret_out_next is None else None
            pm, ps, *made = _ret_proj(xm_bf, xs_bf, w_ret_in, j, rope_m, rope_s, lp, job)
            am, a_meta, a_sample, ret_p, ret_s = _ret_mixer(pm, ps, state_ret, j, ret_p, ret_s, tabs,
                                                            bp, lp, bs, ls)
            a_small = jnp.concatenate([a_sample, a_meta], axis=0)
            w_out = made[0] if made else w_ret_out_next
            w_ret_out_next = None
        else:
            jobs = [(w_conv_out, j)]
            if j + 1 < w_ret_out.shape[0]:
                jobs.append((w_ret_out, j + 1))
            am, a_small, sp, ss, w_out, *nxt = _conv_mixer(xm_bf, xs_bf, w_conv_in, conv_w, j, state_conv,
                                                           jobs, bp, lp, bs, ls)
            if nxt:
                w_ret_out_next = nxt[0]
            conv_p.append(sp)
            conv_s.append(ss)
        xm, xm_bf = _out_ln(am, w_out, xm, g_i, b_i, OUT_TM, OUT_SUB)
        if i + 1 < DEPTH:
            xs, xs_bf = _out_ln(a_small, w_out, xs, g_i, b_i, xs.shape[0] // 2, xs.shape[0] // 4)
        else:
            xs, _ = _out_ln(a_small, w_out, xs, g_i, b_i, OUT_TM, OUT_SUB, rows=n_sample_rows)

    y_prompt = xm.reshape(bp, lp, d)
    y_sample = xs.reshape(bs, ls, d)
    return (y_prompt, y_sample, ret_p, ret_s, jnp.stack(conv_p), jnp.stack(conv_s))
```

```python
import functools

import jax
import jax.numpy as jnp
from jax import lax
from jax.experimental import pallas as pl
from jax.experimental.pallas import tpu as pltpu

D_MODEL = 2048
DEPTH = 4
PAST_LEN = 16384
N_META = 16
RET_HEADS = 8
RET_DK = D_MODEL // RET_HEADS
RET_DV = 2 * RET_DK
RET_QK = RET_HEADS * RET_DK
RET_V = RET_HEADS * RET_DV
CONV_WIDTH = 3
CONV_E = D_MODEL
ROPE_BASE = 10000.0
LN_EPS = 1e-5
GN_EPS = 1e-6
DEEPNORM_ALPHA = (2.0 * DEPTH) ** 0.25

HALF = RET_DK // 2
SUBLANES = 8
PROMPT_CHUNK = 256
SAMPLE_GROUP = 2
RET_HEAD_GROUP = 2
RET_PARTS = 4
MAIN_TILE = 1024
PROJ_TN = 1024
CONV_TN = 256
PROJ_SUBBLOCKS = 4
PROJ_SUB_ROWS = 256
OUT_TM = 512
OUT_SUB = (256, 256)
OUT_STREAM_TK = 1024
STATE_SLOTS = 3
VMEM_LIMIT = 56 * 1024 * 1024
VMEM_LIMIT_HIGH = 60 * 1024 * 1024

BF16 = jnp.bfloat16
F32 = jnp.float32


def _params(*sem):
    return pltpu.CompilerParams(dimension_semantics=sem, vmem_limit_bytes=VMEM_LIMIT)


def _dot(a, b):
    return jnp.dot(a, b, preferred_element_type=F32)


def _dot_nt(a, b):
    return lax.dot_general(a, b, (((1,), (1,)), ((), ())), preferred_element_type=F32)


def _dot_tn(a, b):
    return lax.dot_general(a, b, (((0,), (0,)), ((), ())), preferred_element_type=F32)


def _silu(x):
    return x * (1.0 / (1.0 + jnp.exp(-x)))


def _proj_kernel(xm_ref, xs_ref, w_ref, cm_ref, sm_ref, cs_ref, ss_ref, om_ref, os_ref, wbf_ref, *,
                 n_main, tn):
    j = pl.program_id(0)
    i = pl.program_id(1)
    rope_tiles = 2 * RET_QK // tn
    q_tiles = RET_QK // tn

    @pl.when(i == 0)
    def _():
        wbf_ref[...] = w_ref[...].astype(BF16)

    is_rope = j < rope_tiles
    scale = jnp.where(j < q_tiles, RET_DK ** -0.5, 1.0).astype(F32)

    def path(x_ref, c_ref, s_ref, o_ref):
        n_sub = max(PROJ_SUBBLOCKS, x_ref.shape[0] // PROJ_SUB_ROWS)
        sub = x_ref.shape[0] // n_sub
        for r in range(n_sub):
            rows = slice(r * sub, (r + 1) * sub)
            acc = _dot(x_ref[rows, :].astype(BF16), wbf_ref[...])
            c = c_ref[rows, :] * scale
            s = s_ref[rows, :] * scale
            for h in range(tn // RET_DK):
                lo = slice(h * RET_DK, h * RET_DK + HALF)
                hi = slice(h * RET_DK + HALF, (h + 1) * RET_DK)
                t1 = acc[:, lo]
                t2 = acc[:, hi]
                o_ref[rows, lo] = jnp.where(is_rope, t1 * c - t2 * s, t1).astype(o_ref.dtype)
                o_ref[rows, hi] = jnp.where(is_rope, t1 * s + t2 * c, t2).astype(o_ref.dtype)

    @pl.when(i < n_main)
    def _():
        path(xm_ref, cm_ref, sm_ref, om_ref)

    @pl.when(i == n_main)
    def _():
        path(xs_ref, cs_ref, ss_ref, os_ref)


def _ret_proj(xm, xs, w_all, layer, rope_m, rope_s, seq):
    rm, k = xm.shape
    rs = xs.shape[0]
    n = w_all.shape[2]
    tn = PROJ_TN
    tm = MAIN_TILE if xm.dtype == F32 else 2 * MAIN_TILE
    n_main = rm // tm
    tiles_per_seq = seq // tm
    last = n_main - 1
    in_specs = [pl.BlockSpec((tm, k), lambda j, i: (jnp.minimum(i, last), 0)),
                pl.BlockSpec((rs, k), lambda j, i: (0, 0)),
                pl.BlockSpec((None, k, tn), lambda j, i: (layer, 0, j)),
                pl.BlockSpec((tm, HALF), lambda j, i: (jnp.minimum(i, last) % tiles_per_seq, 0)),
                pl.BlockSpec((tm, HALF), lambda j, i: (jnp.minimum(i, last) % tiles_per_seq, 0)),
                pl.BlockSpec((rs, HALF), lambda j, i: (0, 0)),
                pl.BlockSpec((rs, HALF), lambda j, i: (0, 0))]
    out_specs = [pl.BlockSpec((tm, tn), lambda j, i: (jnp.minimum(i, last), j)),
                 pl.BlockSpec((rs, tn), lambda j, i: (0, j))]
    out_shape = [jax.ShapeDtypeStruct((rm, n), BF16), jax.ShapeDtypeStruct((rs, n), BF16)]
    args = [xm, xs, w_all, rope_m[0], rope_m[1], rope_s[0], rope_s[1]]
    kern = functools.partial(_proj_kernel, n_main=n_main, tn=tn)
    return pl.pallas_call(
        kern, grid=(n // tn, n_main + 1),
        in_specs=in_specs, out_specs=out_specs, out_shape=out_shape,
        scratch_shapes=[pltpu.VMEM((k, tn), BF16)],
        compiler_params=pltpu.CompilerParams(dimension_semantics=("arbitrary", "arbitrary"),
                                             vmem_limit_bytes=VMEM_LIMIT_HIGH),
        name="ret_proj")(*args)


def _out_ln_kernel(a_ref, w_ref, x_ref, g_ref, b_ref, y_ref, ybf_ref, *, sub):
    r0 = 0
    for n_rows in sub:
        rows = slice(r0, r0 + n_rows)
        r0 += n_rows
        y = DEEPNORM_ALPHA * x_ref[rows, :] + _dot(a_ref[rows, :], w_ref[...])
        mu = jnp.mean(y, axis=-1, keepdims=True)
        d = y - mu
        var = jnp.mean(d * d, axis=-1, keepdims=True)
        out = d * lax.rsqrt(var + LN_EPS) * g_ref[...] + b_ref[...]
        y_ref[rows, :] = out
        ybf_ref[rows, :] = out.astype(BF16)


def _out_ln(a, w, x, g, b, tm, sub, rows=None):
    m, k = a.shape
    m = m if rows is None else rows
    n = w.shape[1]
    kern = functools.partial(_out_ln_kernel, sub=sub)
    return pl.pallas_call(
        kern, grid=(m // tm,),
        in_specs=[pl.BlockSpec((tm, k), lambda i: (i, 0)),
                  pl.BlockSpec((k, n), lambda i: (0, 0), pipeline_mode=pl.Buffered(1)),
                  pl.BlockSpec((tm, n), lambda i: (i, 0)),
                  pl.BlockSpec((1, n), lambda i: (0, 0)),
                  pl.BlockSpec((1, n), lambda i: (0, 0))],
        out_specs=[pl.BlockSpec((tm, n), lambda i: (i, 0)),
                   pl.BlockSpec((tm, n), lambda i: (i, 0))],
        out_shape=[jax.ShapeDtypeStruct((m, n), F32), jax.ShapeDtypeStruct((m, n), BF16)],
        compiler_params=_params("parallel"),
        name="out_ln")(a, w, x, g, b)


def _out_ln_stream_kernel(a_ref, w_ref, x_ref, g_ref, b_ref, y_ref, ybf_ref, acc_ref):
    kk = pl.program_id(1)
    part = _dot(a_ref[...], w_ref[...])

    @pl.when(kk == 0)
    def _():
        acc_ref[...] = DEEPNORM_ALPHA * x_ref[...] + part

    @pl.when(kk > 0)
    def _():
        acc_ref[...] += part

    @pl.when(kk == pl.num_programs(1) - 1)
    def _():
        y = acc_ref[...]
        mu = jnp.mean(y, axis=-1, keepdims=True)
        d = y - mu
        var = jnp.mean(d * d, axis=-1, keepdims=True)
        out = d * lax.rsqrt(var + LN_EPS) * g_ref[...] + b_ref[...]
        y_ref[...] = out
        ybf_ref[...] = out.astype(BF16)


def _out_ln_stream(a, w, x, g, b, tm, rows=None):
    m, k = a.shape
    m = m if rows is None else rows
    n = w.shape[1]
    tk = OUT_STREAM_TK
    return pl.pallas_call(
        _out_ln_stream_kernel, grid=(m // tm, k // tk),
        in_specs=[pl.BlockSpec((tm, tk), lambda i, kk: (i, kk)),
                  pl.BlockSpec((tk, n), lambda i, kk: (kk, 0)),
                  pl.BlockSpec((tm, n), lambda i, kk: (i, 0)),
                  pl.BlockSpec((1, n), lambda i, kk: (0, 0)),
                  pl.BlockSpec((1, n), lambda i, kk: (0, 0))],
        out_specs=[pl.BlockSpec((tm, n), lambda i, kk: (i, 0)),
                   pl.BlockSpec((tm, n), lambda i, kk: (i, 0))],
        out_shape=[jax.ShapeDtypeStruct((m, n), F32), jax.ShapeDtypeStruct((m, n), BF16)],
        scratch_shapes=[pltpu.VMEM((tm, n), F32)],
        compiler_params=_params("parallel", "arbitrary"),
        name="out_ln_stream")(a, w, x, g, b)


def _group_norm_gate(y, g):
    mu = jnp.mean(y, axis=-1, keepdims=True)
    d = y - mu
    var = jnp.mean(d * d, axis=-1, keepdims=True)
    return _silu(g.astype(F32)) * (d * lax.rsqrt(var + GN_EPS))


def _ret_mixer_kernel(sdp_ref, sds_ref, q_ref, k_ref, v_ref, g_ref, qm_ref, km_ref, vm_ref, gm_ref,
                      ps_ref, st_hbm, dm_ref, qd_ref, kd_ref, kdm_ref, dms_ref, qds_ref, kds_ref,
                      *rest, n_chunks, sample_seq, layer):
    o_ref, om_ref, osmp_ref, sp_out_ref, ss_out_ref, s_ref, y_ref, st_buf, st_sem = rest[-9:]
    hg = pl.program_id(0)
    b = pl.program_id(1)
    part = pl.program_id(2)
    c_len = PROMPT_CHUNK

    step = (hg * pl.num_programs(1) + b) * pl.num_programs(2) + part
    n_steps = pl.num_programs(0) * pl.num_programs(1) * pl.num_programs(2)

    def state_copy(s_idx):
        slot = s_idx % STATE_SLOTS
        src = st_hbm.at[layer, pl.ds(s_idx * SAMPLE_GROUP, SAMPLE_GROUP)]
        return pltpu.make_async_copy(src, st_buf.at[slot], st_sem.at[slot])

    @pl.when(step == 0)
    def _():
        for s0 in range(STATE_SLOTS - 1):
            state_copy(s0).start()

    @pl.when(step + (STATE_SLOTS - 1) < n_steps)
    def _():
        state_copy(step + (STATE_SLOTS - 1)).start()

    state_copy(step).wait()
    st_ref = st_buf.at[step % STATE_SLOTS]

    def qk_cols(e):
        return slice(e * RET_DK, (e + 1) * RET_DK)

    def v_cols(e):
        return slice(e * RET_DV, (e + 1) * RET_DV)

    @pl.when(part == 0)
    def _():
        r_meta = pl.multiple_of(b * N_META, N_META)
        for e in range(RET_HEAD_GROUP):
            qm = qm_ref[:, qk_cols(e)]
            km = km_ref[:, qk_cols(e)]
            vm = vm_ref[:, v_cols(e)]
            p = (_dot_nt(qm, km) * dm_ref[e, 0:N_META, 0:N_META]).astype(BF16)
            om_ref[pl.ds(r_meta, N_META), v_cols(e)] = _group_norm_gate(
                _dot(p, vm), gm_ref[:, v_cols(e)]).astype(om_ref.dtype)
            kdk = (km.astype(F32) * kdm_ref[e]).astype(BF16)
            s_ref[e] = _dot_tn(kdk, vm)

    def prompt_chunk(e, c):
        rows = slice(c * c_len, (c + 1) * c_len)
        qc = q_ref[rows, qk_cols(e)]
        kc = k_ref[rows, qk_cols(e)]
        vc = v_ref[rows, v_cols(e)]
        s = s_ref[e]
        p = (_dot_nt(qc, kc) * dm_ref[e]).astype(BF16)
        y = _dot(p, vc) + _dot(qc, s.astype(BF16)) * qd_ref[e]
        o_ref[rows, v_cols(e)] = _group_norm_gate(y, g_ref[rows, v_cols(e)]).astype(o_ref.dtype)
        kdk = (kc.astype(F32) * kd_ref[e]).astype(BF16)
        s_ref[e] = s * sdp_ref[hg * RET_HEAD_GROUP + e] + _dot_tn(kdk, vc)

    pf = ps_ref[...].astype(F32)

    def sample_head(gi, hh):
        rows = slice(gi * sample_seq, (gi + 1) * sample_seq)
        qb = pf[rows, hh * RET_DK:(hh + 1) * RET_DK].astype(BF16)
        k = pf[rows, RET_QK + hh * RET_DK:RET_QK + (hh + 1) * RET_DK]
        vb = pf[rows, 2 * RET_QK + hh * RET_DV:2 * RET_QK + (hh + 1) * RET_DV].astype(BF16)
        s = st_ref[gi, hh]
        p = (_dot_nt(qb, k.astype(BF16)) * dms_ref[hh]).astype(BF16)
        y_ref[rows, hh * RET_DV:(hh + 1) * RET_DV] = (
            _dot(p, vb) + _dot(qb, s.astype(BF16)) * qds_ref[hh])
        kdk = (k * kds_ref[hh]).astype(BF16)
        ss_out_ref[gi, hh] = s * sds_ref[hh] + _dot_tn(kdk, vb)

    pairs = [(gi, hh) for gi in range(SAMPLE_GROUP) for hh in range(RET_HEADS)]
    bodies = [(e, c) for c in range(n_chunks) for e in range(RET_HEAD_GROUP)]
    per_body = len(pairs) // len(bodies)
    for n, (e, c) in enumerate(bodies):
        prompt_chunk(e, c)
        for gi, hh in pairs[n * per_body:(n + 1) * per_body]:
            sample_head(gi, hh)
    for gi, hh in pairs[len(bodies) * per_body:]:
        sample_head(gi, hh)
    g0 = 2 * RET_QK + RET_V
    for hh in range(RET_HEADS):
        cols = slice(hh * RET_DV, (hh + 1) * RET_DV)
        g = pf[:, g0 + hh * RET_DV:g0 + (hh + 1) * RET_DV]
        osmp_ref[:, cols] = _group_norm_gate(y_ref[:, cols], g).astype(osmp_ref.dtype)

    @pl.when(part == pl.num_programs(2) - 1)
    def _():
        sp_out_ref[...] = s_ref[...]


def _ret_mixer(pm, ps, state_all, layer, new_p_all, new_s_all, tabs, batch, seq, n_sample, sample_seq):
    c_len = PROMPT_CHUNK
    hd = RET_HEADS
    hgs = RET_HEAD_GROUP
    n_hg = hd // hgs
    parts = RET_PARTS
    rows = seq // parts
    grows = SAMPLE_GROUP * sample_seq
    assert n_hg * batch * parts * SAMPLE_GROUP == n_sample
    n_layers = state_all.shape[0]
    meta0 = n_sample * sample_seq // N_META
    qkw, vw = hgs * RET_DK, hgs * RET_DV
    k_blocks = RET_QK // qkw
    v_blocks = 2 * RET_QK // vw
    g_blocks = v_blocks + RET_V // vw

    def grp(h, b, p):
        return (h * batch + b) * parts + p

    kern = functools.partial(_ret_mixer_kernel, n_chunks=rows // c_len, sample_seq=sample_seq, layer=layer)
    p_state_spec = pl.BlockSpec((None, None, hgs, RET_DK, RET_DV), lambda h, b, p: (layer, b, h, 0, 0))
    s_state_spec = pl.BlockSpec((None, SAMPLE_GROUP, hd, RET_DK, RET_DV),
                                lambda h, b, p: (layer, grp(h, b, p), 0, 0, 0))
    smem = pl.BlockSpec(memory_space=pltpu.SMEM)
    in_specs = [smem, smem,
                pl.BlockSpec((rows, qkw), lambda h, b, p: (b * parts + p, h)),
                pl.BlockSpec((rows, qkw), lambda h, b, p: (b * parts + p, k_blocks + h)),
                pl.BlockSpec((rows, vw), lambda h, b, p: (b * parts + p, v_blocks + h)),
                pl.BlockSpec((rows, vw), lambda h, b, p: (b * parts + p, g_blocks + h)),
                pl.BlockSpec((N_META, qkw), lambda h, b, p: (meta0 + b, h)),
                pl.BlockSpec((N_META, qkw), lambda h, b, p: (meta0 + b, k_blocks + h)),
                pl.BlockSpec((N_META, vw), lambda h, b, p: (meta0 + b, v_blocks + h)),
                pl.BlockSpec((N_META, vw), lambda h, b, p: (meta0 + b, g_blocks + h)),
                pl.BlockSpec((grows, ps.shape[1]), lambda h, b, p: (grp(h, b, p), 0)),
                pl.BlockSpec(memory_space=pl.ANY),
                pl.BlockSpec((hgs, c_len, c_len), lambda h, b, p: (h, 0, 0)),
                pl.BlockSpec((hgs, c_len, 1), lambda h, b, p: (h, 0, 0)),
                pl.BlockSpec((hgs, c_len, 1), lambda h, b, p: (h, 0, 0)),
                pl.BlockSpec((hgs, N_META, 1), lambda h, b, p: (h, 0, 0)),
                pl.BlockSpec((hd, sample_seq, sample_seq), lambda h, b, p: (0, 0, 0)),
                pl.BlockSpec((hd, sample_seq, 1), lambda h, b, p: (0, 0, 0)),
                pl.BlockSpec((hd, sample_seq, 1), lambda h, b, p: (0, 0, 0))]
    args = [tabs["sdec_p"], tabs["sdec_s"], pm, pm, pm, pm, ps, ps, ps, ps, ps, state_all,
            tabs["dm_p"], tabs["qd_p"], tabs["kd_p"], tabs["kd_m"], tabs["dm_s"], tabs["qd_s"], tabs["kd_s"]]
    aliases = {}
    if new_p_all is not None:
        in_specs += [pl.BlockSpec(memory_space=pl.ANY), pl.BlockSpec(memory_space=pl.ANY)]
        args += [new_p_all, new_s_all]
        aliases = {len(args) - 2: 3, len(args) - 1: 4}
    return pl.pallas_call(
        kern, grid=(n_hg, batch, parts),
        in_specs=in_specs,
        out_specs=[pl.BlockSpec((rows, vw), lambda h, b, p: (b * parts + p, h)),
                   pl.BlockSpec((batch * N_META, vw), lambda h, b, p: (0, h)),
                   pl.BlockSpec((grows, RET_V), lambda h, b, p: (grp(h, b, p), 0)),
                   p_state_spec, s_state_spec],
        out_shape=[jax.ShapeDtypeStruct((batch * seq, RET_V), BF16),
                   jax.ShapeDtypeStruct((batch * N_META, RET_V), BF16),
                   jax.ShapeDtypeStruct((n_sample * sample_seq, RET_V), BF16),
                   jax.ShapeDtypeStruct((n_layers, batch, hd, RET_DK, RET_DV), F32),
                   jax.ShapeDtypeStruct(state_all.shape, F32)],
        scratch_shapes=[pltpu.VMEM((hgs, RET_DK, RET_DV), F32), pltpu.VMEM((grows, RET_V), F32),
                        pltpu.VMEM((STATE_SLOTS, SAMPLE_GROUP, hd, RET_DK, RET_DV), F32),
                        pltpu.SemaphoreType.DMA((STATE_SLOTS,))],
        input_output_aliases=aliases,
        compiler_params=pltpu.CompilerParams(dimension_semantics=("arbitrary",) * 3,
                                             vmem_limit_bytes=VMEM_LIMIT_HIGH),
        name="ret_mixer")(*args)


def _gated_conv(u, zb, cw_ref, prev1, prev2, t):
    u1 = jnp.where(t >= 1, pltpu.roll(u, 1, 0), prev1)
    u2 = jnp.where(t >= 2, pltpu.roll(u, 2, 0), jnp.where(t == 1, prev1, prev2))
    conv = cw_ref[0:1, :] * u2 + cw_ref[1:2, :] * u1 + cw_ref[2:3, :] * u
    return zb * conv


def _conv_kernel(*refs, n_cast, n_sample, sample_seq, n_prompt, tiles_per_seq):
    (xm_ref, xs_ref, wb_ref, wc_ref, wv_ref, wz_ref, cw_ref, st_ref), refs = refs[:8], refs[8:]
    cast_in, refs = refs[:n_cast], refs[n_cast:]
    (om_ref, os_ref, stp_ref, sts_ref), refs = refs[:4], refs[4:]
    cast_out, (wbf_ref, meta_tail_ref, tail_ref) = refs[:n_cast], refs[n_cast:]
    t_step = pl.program_id(1)
    tn = om_ref.shape[1]
    n_sample_rows = n_sample * sample_seq

    for n, (src_ref, dst_ref) in enumerate(zip(cast_in, cast_out)):
        @pl.when(t_step <= n)
        def _(src_ref=src_ref, dst_ref=dst_ref):
            dst_ref[...] = src_ref[...].astype(BF16)

    def project(x):
        u = _dot(x, wbf_ref[1]) * _dot(x, wbf_ref[2])
        zb = _silu(_dot(x, wbf_ref[3])) * _dot(x, wbf_ref[0])
        return u, zb

    @pl.when(t_step == 0)
    def _():
        wbf_ref[0] = wb_ref[...].astype(BF16)
        wbf_ref[1] = wc_ref[...].astype(BF16)
        wbf_ref[2] = wv_ref[...].astype(BF16)
        wbf_ref[3] = wz_ref[...].astype(BF16)
        u, zb = project(xs_ref[...])
        us = u[0:n_sample_rows]
        st = st_ref[...]
        shape3 = (n_sample, sample_seq, tn)
        prev2 = jnp.broadcast_to(st[:, 0:1, :], shape3).reshape(n_sample_rows, tn)
        prev1 = jnp.broadcast_to(st[:, 1:2, :], shape3).reshape(n_sample_rows, tn)
        t = lax.broadcasted_iota(jnp.int32, us.shape, 0) % sample_seq
        os_ref[0:n_sample_rows, :] = _gated_conv(us, zb[0:n_sample_rows], cw_ref, prev1, prev2,
                                                 t).astype(os_ref.dtype)
        sts_ref[...] = us.reshape(shape3)[:, sample_seq - (CONV_WIDTH - 1):sample_seq, :]
        um = u[n_sample_rows:]
        t = lax.broadcasted_iota(jnp.int32, um.shape, 0) % N_META
        zero = jnp.zeros((), F32)
        os_ref[n_sample_rows:, :] = _gated_conv(um, zb[n_sample_rows:], cw_ref, zero, zero,
                                                t).astype(os_ref.dtype)
        for bb in range(n_prompt):
            meta_tail_ref[bb] = um[(bb + 1) * N_META - SUBLANES:(bb + 1) * N_META]

    @pl.when(t_step > 0)
    def _():
        tile = t_step - 1
        bb = tile // tiles_per_seq
        first = tile % tiles_per_seq == 0
        u, zb = project(xm_ref[...])
        rows = u.shape[0]
        tail = jnp.where(first, meta_tail_ref[bb], tail_ref[...])
        prev1 = tail[SUBLANES - 1:SUBLANES, :]
        prev2 = tail[SUBLANES - 2:SUBLANES - 1, :]
        t = lax.broadcasted_iota(jnp.int32, u.shape, 0)
        om_ref[...] = _gated_conv(u, zb, cw_ref, prev1, prev2, t).astype(om_ref.dtype)
        tail_ref[...] = u[rows - SUBLANES:rows]

        @pl.when(tile % tiles_per_seq == tiles_per_seq - 1)
        def _():
            stp_ref[0] = u[rows - (CONV_WIDTH - 1):rows]


def _conv_mixer(xm, xs, w_all, cw_all, layer, state_all, cast_jobs, n_prompt, seq, n_sample, sample_seq):
    rm, k = xm.shape
    rs = xs.shape[0]
    e = CONV_E
    tn, tm = CONV_TN, MAIN_TILE
    nt = e // tn
    n_main = rm // tm
    tiles_per_seq = seq // tm
    ns = CONV_WIDTH - 1

    def main_tile(t):
        return jnp.maximum(t - 1, 0)

    w_specs = [pl.BlockSpec((None, k, tn), functools.partial(lambda j, t, q: (layer, 0, q * nt + j), q=q))
               for q in range(4)]
    cast_in_specs, cast_out_specs, cast_shapes, cast_args = [], [], [], []
    for n, (w_src, src_layer) in enumerate(cast_jobs):
        _, wk, wn = w_src.shape
        rows = wk // (nt * (n + 1))

        def blk(j, t, n=n):
            return j * (n + 1) + jnp.minimum(t, n)

        cast_in_specs.append(pl.BlockSpec((None, rows, wn),
                                          functools.partial(lambda j, t, f, l: (l, f(j, t), 0), f=blk, l=src_layer)))
        cast_out_specs.append(pl.BlockSpec((rows, wn), functools.partial(lambda j, t, f: (f(j, t), 0), f=blk)))
        cast_shapes.append(jax.ShapeDtypeStruct((wk, wn), BF16))
        cast_args.append(w_src)
    kern = functools.partial(_conv_kernel, n_cast=len(cast_jobs), n_sample=n_sample, sample_seq=sample_seq,
                             n_prompt=n_prompt, tiles_per_seq=tiles_per_seq)
    return pl.pallas_call(
        kern, grid=(nt, n_main + 1),
        in_specs=[pl.BlockSpec((tm, k), lambda j, t: (main_tile(t), 0)),
                  pl.BlockSpec((rs, k), lambda j, t: (0, 0))] + w_specs + [
                  pl.BlockSpec((None, CONV_WIDTH, tn), lambda j, t: (layer, 0, j)),
                  pl.BlockSpec((None, n_sample, ns, tn), lambda j, t: (layer, 0, 0, j))] + cast_in_specs,
        out_specs=[pl.BlockSpec((tm, tn), lambda j, t: (main_tile(t), j)),
                   pl.BlockSpec((rs, tn), lambda j, t: (0, j)),
                   pl.BlockSpec((1, ns, tn), lambda j, t: (main_tile(t) // tiles_per_seq, 0, j)),
                   pl.BlockSpec((n_sample, ns, tn), lambda j, t: (0, 0, j))] + cast_out_specs,
        out_shape=[jax.ShapeDtypeStruct((rm, e), BF16),
                   jax.ShapeDtypeStruct((rs, e), BF16),
                   jax.ShapeDtypeStruct((n_prompt, ns, e), F32),
                   jax.ShapeDtypeStruct((n_sample, ns, e), F32)] + cast_shapes,
        scratch_shapes=[pltpu.VMEM((4, k, tn), BF16),
                        pltpu.VMEM((n_prompt, SUBLANES, tn), F32),
                        pltpu.VMEM((SUBLANES, tn), F32)],
        compiler_params=_params("arbitrary", "arbitrary"),
        name="conv_mixer")(xm, xs, w_all, w_all, w_all, w_all, cw_all, state_all, *cast_args)


def _rope_tables(pos):
    freqs = ROPE_BASE ** (-jnp.arange(HALF, dtype=F32) * 2.0 / RET_DK)
    ang = pos.astype(F32)[:, None] * freqs[None, :]
    return jnp.cos(ang), jnp.sin(ang)


def _decay_tables(length):
    lg = jnp.log1p(-jnp.exp2(-5.0 - jnp.arange(RET_HEADS, dtype=F32)))
    idx = jnp.arange(length, dtype=F32)
    diff = idx[:, None] - idx[None, :]
    dmask = jnp.exp(jnp.where(diff[None] >= 0, diff[None] * lg[:, None, None], -jnp.inf))
    q_decay = jnp.exp((idx + 1.0)[None, :] * lg[:, None])[:, :, None]
    k_decay = jnp.exp((length - 1.0 - idx)[None, :] * lg[:, None])[:, :, None]
    s_decay = jnp.exp(length * lg)
    return dmask, q_decay, k_decay, s_decay


def kernel(x_prompt, x_sample, state_ret, state_conv, meta, w_ret_in, w_ret_out, w_conv_in, conv_w,
           w_conv_out, ln_g, ln_b):
    bp, lp, d = x_prompt.shape
    bs, ls, _ = x_sample.shape
    n_sample_rows = bs * ls

    xm = x_prompt.reshape(bp * lp, d)
    xs = jnp.concatenate([x_sample.reshape(n_sample_rows, d),
                          jnp.tile(meta.astype(x_prompt.dtype), (bp, 1))], axis=0)
    xm_bf, xs_bf = xm, xs.astype(BF16)

    rope_m = _rope_tables(N_META + jnp.arange(lp, dtype=jnp.int32))
    pos_s = jnp.concatenate([jnp.tile(PAST_LEN + jnp.arange(ls, dtype=jnp.int32), bs),
                             jnp.tile(jnp.arange(N_META, dtype=jnp.int32), bp)])
    rope_s = _rope_tables(pos_s)

    dm_p, qd_p, kd_p, sdec_p = _decay_tables(PROMPT_CHUNK)
    _, _, kd_m, _ = _decay_tables(N_META)
    dm_s, qd_s, kd_s, sdec_s = _decay_tables(ls)
    tabs = dict(dm_p=dm_p, qd_p=qd_p, kd_p=kd_p, sdec_p=sdec_p, kd_m=kd_m,
                dm_s=dm_s, qd_s=qd_s, kd_s=kd_s, sdec_s=sdec_s)

    w_ret_out_next = w_ret_out[0].astype(BF16)

    ret_p = ret_s = None
    conv_p, conv_s = [], []
    for i in range(DEPTH):
        j = i // 2
        g_i, b_i = ln_g[i][None, :], ln_b[i][None, :]
        if i % 2 == 0:
            pm, ps = _ret_proj(xm_bf, xs_bf, w_ret_in, j, rope_m, rope_s, lp)
            am, a_meta, a_sample, ret_p, ret_s = _ret_mixer(pm, ps, state_ret, j, ret_p, ret_s, tabs,
                                                            bp, lp, bs, ls)
            a_small = jnp.concatenate([a_sample, a_meta], axis=0)
            w_out = w_ret_out_next
        else:
            jobs = [(w_conv_out, j)]
            if j + 1 < w_ret_out.shape[0]:
                jobs.append((w_ret_out, j + 1))
            am, a_small, sp, ss, w_out, *nxt = _conv_mixer(xm_bf, xs_bf, w_conv_in, conv_w, j, state_conv,
                                                           jobs, bp, lp, bs, ls)
            if nxt:
                w_ret_out_next = nxt[0]
            conv_p.append(sp)
            conv_s.append(ss)
        xm, xm_bf = _out_ln(am, w_out, xm, g_i, b_i, OUT_TM, OUT_SUB)
        if i + 1 < DEPTH:
            xs, xs_bf = _out_ln_stream(a_small, w_out, xs, g_i, b_i, xs.shape[0] // 2)
        else:
            xs, _ = _out_ln_stream(a_small, w_out, xs, g_i, b_i, OUT_TM, rows=n_sample_rows)

    y_prompt = xm.reshape(bp, lp, d)
    y_sample = xs.reshape(bs, ls, d)
    return (y_prompt, y_sample, ret_p, ret_s, jnp.stack(conv_p), jnp.stack(conv_s))
```

```python
import functools

import jax
import jax.numpy as jnp
from jax import lax
from jax.experimental import pallas as pl
from jax.experimental.pallas import tpu as pltpu

D_MODEL = 2048
DEPTH = 4
PAST_LEN = 16384
N_META = 16
RET_HEADS = 8
RET_DK = D_MODEL // RET_HEADS
RET_DV = 2 * RET_DK
RET_QK = RET_HEADS * RET_DK
RET_V = RET_HEADS * RET_DV
CONV_WIDTH = 3
CONV_E = D_MODEL
ROPE_BASE = 10000.0
LN_EPS = 1e-5
GN_EPS = 1e-6
DEEPNORM_ALPHA = (2.0 * DEPTH) ** 0.25

HALF = RET_DK // 2
SUBLANES = 8
PROMPT_CHUNK = 256
SAMPLE_GROUP = 2
RET_HEAD_GROUP = 2
RET_PARTS = 4
MAIN_TILE = 1024
PROJ_TN = 1024
CONV_TN = 256
PROJ_SUBBLOCKS = 4
PROJ_SUB_ROWS = 256
OUT_TM = 512
OUT_SUB = (256, 256)
STATE_SLOTS = 3
VMEM_LIMIT = 56 * 1024 * 1024
VMEM_LIMIT_HIGH = 60 * 1024 * 1024

BF16 = jnp.bfloat16
F32 = jnp.float32


def _params(*sem):
    return pltpu.CompilerParams(dimension_semantics=sem, vmem_limit_bytes=VMEM_LIMIT)


def _dot(a, b):
    return jnp.dot(a, b, preferred_element_type=F32)


def _dot_nt(a, b):
    return lax.dot_general(a, b, (((1,), (1,)), ((), ())), preferred_element_type=F32)


def _dot_tn(a, b):
    return lax.dot_general(a, b, (((0,), (0,)), ((), ())), preferred_element_type=F32)


def _silu(x):
    return x * (1.0 / (1.0 + jnp.exp(-x)))


def _proj_kernel(xm_ref, xs_ref, w_ref, cm_ref, sm_ref, cs_ref, ss_ref, om_ref, os_ref, wbf_ref, *,
                 n_main, tn):
    j = pl.program_id(0)
    i = pl.program_id(1)
    rope_tiles = 2 * RET_QK // tn
    q_tiles = RET_QK // tn

    @pl.when(i == 0)
    def _():
        wbf_ref[...] = w_ref[...].astype(BF16)

    is_rope = j < rope_tiles
    scale = jnp.where(j < q_tiles, RET_DK ** -0.5, 1.0).astype(F32)

    def path(x_ref, c_ref, s_ref, o_ref):
        n_sub = max(PROJ_SUBBLOCKS, x_ref.shape[0] // PROJ_SUB_ROWS)
        sub = x_ref.shape[0] // n_sub
        for r in range(n_sub):
            rows = slice(r * sub, (r + 1) * sub)
            acc = _dot(x_ref[rows, :].astype(BF16), wbf_ref[...])
            c = c_ref[rows, :] * scale
            s = s_ref[rows, :] * scale
            for h in range(tn // RET_DK):
                lo = slice(h * RET_DK, h * RET_DK + HALF)
                hi = slice(h * RET_DK + HALF, (h + 1) * RET_DK)
                t1 = acc[:, lo]
                t2 = acc[:, hi]
                o_ref[rows, lo] = jnp.where(is_rope, t1 * c - t2 * s, t1).astype(o_ref.dtype)
                o_ref[rows, hi] = jnp.where(is_rope, t1 * s + t2 * c, t2).astype(o_ref.dtype)

    @pl.when(i < n_main)
    def _():
        path(xm_ref, cm_ref, sm_ref, om_ref)

    @pl.when(i == n_main)
    def _():
        path(xs_ref, cs_ref, ss_ref, os_ref)


def _ret_proj(xm, xs, w_all, layer, rope_m, rope_s, seq):
    rm, k = xm.shape
    rs = xs.shape[0]
    n = w_all.shape[2]
    tn = PROJ_TN
    tm = MAIN_TILE if xm.dtype == F32 else 2 * MAIN_TILE
    n_main = rm // tm
    tiles_per_seq = seq // tm
    last = n_main - 1
    in_specs = [pl.BlockSpec((tm, k), lambda j, i: (jnp.minimum(i, last), 0)),
                pl.BlockSpec((rs, k), lambda j, i: (0, 0)),
                pl.BlockSpec((None, k, tn), lambda j, i: (layer, 0, j)),
                pl.BlockSpec((tm, HALF), lambda j, i: (jnp.minimum(i, last) % tiles_per_seq, 0)),
                pl.BlockSpec((tm, HALF), lambda j, i: (jnp.minimum(i, last) % tiles_per_seq, 0)),
                pl.BlockSpec((rs, HALF), lambda j, i: (0, 0)),
                pl.BlockSpec((rs, HALF), lambda j, i: (0, 0))]
    out_specs = [pl.BlockSpec((tm, tn), lambda j, i: (jnp.minimum(i, last), j)),
                 pl.BlockSpec((rs, tn), lambda j, i: (0, j))]
    out_shape = [jax.ShapeDtypeStruct((rm, n), BF16), jax.ShapeDtypeStruct((rs, n), BF16)]
    args = [xm, xs, w_all, rope_m[0], rope_m[1], rope_s[0], rope_s[1]]
    kern = functools.partial(_proj_kernel, n_main=n_main, tn=tn)
    return pl.pallas_call(
        kern, grid=(n // tn, n_main + 1),
        in_specs=in_specs, out_specs=out_specs, out_shape=out_shape,
        scratch_shapes=[pltpu.VMEM((k, tn), BF16)],
        compiler_params=pltpu.CompilerParams(dimension_semantics=("arbitrary", "arbitrary"),
                                             vmem_limit_bytes=VMEM_LIMIT_HIGH),
        name="ret_proj")(*args)


def _out_ln_kernel(a_ref, w_ref, x_ref, g_ref, b_ref, y_ref, ybf_ref, *, sub):
    r0 = 0
    for n_rows in sub:
        rows = slice(r0, r0 + n_rows)
        r0 += n_rows
        y = DEEPNORM_ALPHA * x_ref[rows, :] + _dot(a_ref[rows, :], w_ref[...])
        mu = jnp.mean(y, axis=-1, keepdims=True)
        d = y - mu
        var = jnp.mean(d * d, axis=-1, keepdims=True)
        out = d * lax.rsqrt(var + LN_EPS) * g_ref[...] + b_ref[...]
        y_ref[rows, :] = out
        ybf_ref[rows, :] = out.astype(BF16)


def _out_ln(a, w, x, g, b, tm, sub, rows=None):
    m, k = a.shape
    m = m if rows is None else rows
    n = w.shape[1]
    kern = functools.partial(_out_ln_kernel, sub=sub)
    return pl.pallas_call(
        kern, grid=(m // tm,),
        in_specs=[pl.BlockSpec((tm, k), lambda i: (i, 0)),
                  pl.BlockSpec((k, n), lambda i: (0, 0), pipeline_mode=pl.Buffered(1)),
                  pl.BlockSpec((tm, n), lambda i: (i, 0)),
                  pl.BlockSpec((1, n), lambda i: (0, 0)),
                  pl.BlockSpec((1, n), lambda i: (0, 0))],
        out_specs=[pl.BlockSpec((tm, n), lambda i: (i, 0)),
                   pl.BlockSpec((tm, n), lambda i: (i, 0))],
        out_shape=[jax.ShapeDtypeStruct((m, n), F32), jax.ShapeDtypeStruct((m, n), BF16)],
        compiler_params=_params("parallel"),
        name="out_ln")(a, w, x, g, b)


def _group_norm_gate(y, g):
    mu = jnp.mean(y, axis=-1, keepdims=True)
    d = y - mu
    var = jnp.mean(d * d, axis=-1, keepdims=True)
    return _silu(g.astype(F32)) * (d * lax.rsqrt(var + GN_EPS))


def _ret_mixer_kernel(sdp_ref, sds_ref, q_ref, k_ref, v_ref, g_ref, qm_ref, km_ref, vm_ref, gm_ref,
                      ps_ref, st_hbm, dm_ref, qd_ref, kd_ref, kdm_ref, dms_ref, qds_ref, kds_ref,
                      *rest, n_chunks, sample_seq, layer):
    o_ref, om_ref, osmp_ref, sp_out_ref, ss_out_ref, s_ref, y_ref, st_buf, st_sem = rest[-9:]
    hg = pl.program_id(0)
    b = pl.program_id(1)
    part = pl.program_id(2)
    c_len = PROMPT_CHUNK

    step = (hg * pl.num_programs(1) + b) * pl.num_programs(2) + part
    n_steps = pl.num_programs(0) * pl.num_programs(1) * pl.num_programs(2)

    def state_copy(s_idx):
        slot = s_idx % STATE_SLOTS
        src = st_hbm.at[layer, pl.ds(s_idx * SAMPLE_GROUP, SAMPLE_GROUP)]
        return pltpu.make_async_copy(src, st_buf.at[slot], st_sem.at[slot])

    @pl.when(step == 0)
    def _():
        for s0 in range(STATE_SLOTS - 1):
            state_copy(s0).start(priority=1)

    @pl.when(step + (STATE_SLOTS - 1) < n_steps)
    def _():
        state_copy(step + (STATE_SLOTS - 1)).start(priority=1)

    state_copy(step).wait()
    st_ref = st_buf.at[step % STATE_SLOTS]

    def qk_cols(e):
        return slice(e * RET_DK, (e + 1) * RET_DK)

    def v_cols(e):
        return slice(e * RET_DV, (e + 1) * RET_DV)

    @pl.when(part == 0)
    def _():
        r_meta = pl.multiple_of(b * N_META, N_META)
        for e in range(RET_HEAD_GROUP):
            qm = qm_ref[:, qk_cols(e)]
            km = km_ref[:, qk_cols(e)]
            vm = vm_ref[:, v_cols(e)]
            p = (_dot_nt(qm, km) * dm_ref[e, 0:N_META, 0:N_META]).astype(BF16)
            om_ref[pl.ds(r_meta, N_META), v_cols(e)] = _group_norm_gate(
                _dot(p, vm), gm_ref[:, v_cols(e)]).astype(om_ref.dtype)
            kdk = (km.astype(F32) * kdm_ref[e]).astype(BF16)
            s_ref[e] = _dot_tn(kdk, vm)

    def prompt_chunk(e, c):
        rows = slice(c * c_len, (c + 1) * c_len)
        qc = q_ref[rows, qk_cols(e)]
        kc = k_ref[rows, qk_cols(e)]
        vc = v_ref[rows, v_cols(e)]
        s = s_ref[e]
        p = (_dot_nt(qc, kc) * dm_ref[e]).astype(BF16)
        y = _dot(p, vc) + _dot(qc, s.astype(BF16)) * qd_ref[e]
        o_ref[rows, v_cols(e)] = _group_norm_gate(y, g_ref[rows, v_cols(e)]).astype(o_ref.dtype)
        kdk = (kc.astype(F32) * kd_ref[e]).astype(BF16)
        s_ref[e] = s * sdp_ref[hg * RET_HEAD_GROUP + e] + _dot_tn(kdk, vc)

    pf = ps_ref[...].astype(F32)

    def sample_head(gi, hh):
        rows = slice(gi * sample_seq, (gi + 1) * sample_seq)
        qb = pf[rows, hh * RET_DK:(hh + 1) * RET_DK].astype(BF16)
        k = pf[rows, RET_QK + hh * RET_DK:RET_QK + (hh + 1) * RET_DK]
        vb = pf[rows, 2 * RET_QK + hh * RET_DV:2 * RET_QK + (hh + 1) * RET_DV].astype(BF16)
        s = st_ref[gi, hh]
        p = (_dot_nt(qb, k.astype(BF16)) * dms_ref[hh]).astype(BF16)
        y_ref[rows, hh * RET_DV:(hh + 1) * RET_DV] = (
            _dot(p, vb) + _dot(qb, s.astype(BF16)) * qds_ref[hh])
        kdk = (k * kds_ref[hh]).astype(BF16)
        ss_out_ref[gi, hh] = s * sds_ref[hh] + _dot_tn(kdk, vb)

    pairs = [(gi, hh) for gi in range(SAMPLE_GROUP) for hh in range(RET_HEADS)]
    bodies = [(e, c) for c in range(n_chunks) for e in range(RET_HEAD_GROUP)]
    per_body = len(pairs) // len(bodies)
    for n, (e, c) in enumerate(bodies):
        prompt_chunk(e, c)
        for gi, hh in pairs[n * per_body:(n + 1) * per_body]:
            sample_head(gi, hh)
    for gi, hh in pairs[len(bodies) * per_body:]:
        sample_head(gi, hh)
    g0 = 2 * RET_QK + RET_V
    for hh in range(RET_HEADS):
        cols = slice(hh * RET_DV, (hh + 1) * RET_DV)
        g = pf[:, g0 + hh * RET_DV:g0 + (hh + 1) * RET_DV]
        osmp_ref[:, cols] = _group_norm_gate(y_ref[:, cols], g).astype(osmp_ref.dtype)

    @pl.when(part == pl.num_programs(2) - 1)
    def _():
        sp_out_ref[...] = s_ref[...]


def _ret_mixer(pm, ps, state_all, layer, new_p_all, new_s_all, tabs, batch, seq, n_sample, sample_seq):
    c_len = PROMPT_CHUNK
    hd = RET_HEADS
    hgs = RET_HEAD_GROUP
    n_hg = hd // hgs
    parts = RET_PARTS
    rows = seq // parts
    grows = SAMPLE_GROUP * sample_seq
    assert n_hg * batch * parts * SAMPLE_GROUP == n_sample
    n_layers = state_all.shape[0]
    meta0 = n_sample * sample_seq // N_META
    qkw, vw = hgs * RET_DK, hgs * RET_DV
    k_blocks = RET_QK // qkw
    v_blocks = 2 * RET_QK // vw
    g_blocks = v_blocks + RET_V // vw

    def grp(h, b, p):
        return (h * batch + b) * parts + p

    kern = functools.partial(_ret_mixer_kernel, n_chunks=rows // c_len, sample_seq=sample_seq, layer=layer)
    p_state_spec = pl.BlockSpec((None, None, hgs, RET_DK, RET_DV), lambda h, b, p: (layer, b, h, 0, 0))
    s_state_spec = pl.BlockSpec((None, SAMPLE_GROUP, hd, RET_DK, RET_DV),
                                lambda h, b, p: (layer, grp(h, b, p), 0, 0, 0))
    smem = pl.BlockSpec(memory_space=pltpu.SMEM)
    in_specs = [smem, smem,
                pl.BlockSpec((rows, qkw), lambda h, b, p: (b * parts + p, h)),
                pl.BlockSpec((rows, qkw), lambda h, b, p: (b * parts + p, k_blocks + h)),
                pl.BlockSpec((rows, vw), lambda h, b, p: (b * parts + p, v_blocks + h)),
                pl.BlockSpec((rows, vw), lambda h, b, p: (b * parts + p, g_blocks + h)),
                pl.BlockSpec((N_META, qkw), lambda h, b, p: (meta0 + b, h)),
                pl.BlockSpec((N_META, qkw), lambda h, b, p: (meta0 + b, k_blocks + h)),
                pl.BlockSpec((N_META, vw), lambda h, b, p: (meta0 + b, v_blocks + h)),
                pl.BlockSpec((N_META, vw), lambda h, b, p: (meta0 + b, g_blocks + h)),
                pl.BlockSpec((grows, ps.shape[1]), lambda h, b, p: (grp(h, b, p), 0)),
                pl.BlockSpec(memory_space=pl.ANY),
                pl.BlockSpec((hgs, c_len, c_len), lambda h, b, p: (h, 0, 0)),
                pl.BlockSpec((hgs, c_len, 1), lambda h, b, p: (h, 0, 0)),
                pl.BlockSpec((hgs, c_len, 1), lambda h, b, p: (h, 0, 0)),
                pl.BlockSpec((hgs, N_META, 1), lambda h, b, p: (h, 0, 0)),
                pl.BlockSpec((hd, sample_seq, sample_seq), lambda h, b, p: (0, 0, 0)),
                pl.BlockSpec((hd, sample_seq, 1), lambda h, b, p: (0, 0, 0)),
                pl.BlockSpec((hd, sample_seq, 1), lambda h, b, p: (0, 0, 0))]
    args = [tabs["sdec_p"], tabs["sdec_s"], pm, pm, pm, pm, ps, ps, ps, ps, ps, state_all,
            tabs["dm_p"], tabs["qd_p"], tabs["kd_p"], tabs["kd_m"], tabs["dm_s"], tabs["qd_s"], tabs["kd_s"]]
    aliases = {}
    if new_p_all is not None:
        in_specs += [pl.BlockSpec(memory_space=pl.ANY), pl.BlockSpec(memory_space=pl.ANY)]
        args += [new_p_all, new_s_all]
        aliases = {len(args) - 2: 3, len(args) - 1: 4}
    return pl.pallas_call(
        kern, grid=(n_hg, batch, parts),
        in_specs=in_specs,
        out_specs=[pl.BlockSpec((rows, vw), lambda h, b, p: (b * parts + p, h)),
                   pl.BlockSpec((batch * N_META, vw), lambda h, b, p: (0, h)),
                   pl.BlockSpec((grows, RET_V), lambda h, b, p: (grp(h, b, p), 0)),
                   p_state_spec, s_state_spec],
        out_shape=[jax.ShapeDtypeStruct((batch * seq, RET_V), BF16),
                   jax.ShapeDtypeStruct((batch * N_META, RET_V), BF16),
                   jax.ShapeDtypeStruct((n_sample * sample_seq, RET_V), BF16),
                   jax.ShapeDtypeStruct((n_layers, batch, hd, RET_DK, RET_DV), F32),
                   jax.ShapeDtypeStruct(state_all.shape, F32)],
        scratch_shapes=[pltpu.VMEM((hgs, RET_DK, RET_DV), F32), pltpu.VMEM((grows, RET_V), F32),
                        pltpu.VMEM((STATE_SLOTS, SAMPLE_GROUP, hd, RET_DK, RET_DV), F32),
                        pltpu.SemaphoreType.DMA((STATE_SLOTS,))],
        input_output_aliases=aliases,
        compiler_params=pltpu.CompilerParams(dimension_semantics=("arbitrary",) * 3,
                                             vmem_limit_bytes=VMEM_LIMIT_HIGH),
        name="ret_mixer")(*args)


def _gated_conv(u, zb, cw_ref, prev1, prev2, t):
    u1 = jnp.where(t >= 1, pltpu.roll(u, 1, 0), prev1)
    u2 = jnp.where(t >= 2, pltpu.roll(u, 2, 0), jnp.where(t == 1, prev1, prev2))
    conv = cw_ref[0:1, :] * u2 + cw_ref[1:2, :] * u1 + cw_ref[2:3, :] * u
    return zb * conv


def _conv_kernel(*refs, n_cast, n_sample, sample_seq, n_prompt, tiles_per_seq):
    (xm_ref, xs_ref, wb_ref, wc_ref, wv_ref, wz_ref, cw_ref, st_ref), refs = refs[:8], refs[8:]
    cast_in, refs = refs[:n_cast], refs[n_cast:]
    (om_ref, os_ref, stp_ref, sts_ref), refs = refs[:4], refs[4:]
    cast_out, (wbf_ref, meta_tail_ref, tail_ref) = refs[:n_cast], refs[n_cast:]
    t_step = pl.program_id(1)
    tn = om_ref.shape[1]
    n_sample_rows = n_sample * sample_seq

    for n, (src_ref, dst_ref) in enumerate(zip(cast_in, cast_out)):
        @pl.when(t_step <= n)
        def _(src_ref=src_ref, dst_ref=dst_ref):
            dst_ref[...] = src_ref[...].astype(BF16)

    def project(x):
        u = _dot(x, wbf_ref[1]) * _dot(x, wbf_ref[2])
        zb = _silu(_dot(x, wbf_ref[3])) * _dot(x, wbf_ref[0])
        return u, zb

    @pl.when(t_step == 0)
    def _():
        wbf_ref[0] = wb_ref[...].astype(BF16)
        wbf_ref[1] = wc_ref[...].astype(BF16)
        wbf_ref[2] = wv_ref[...].astype(BF16)
        wbf_ref[3] = wz_ref[...].astype(BF16)
        u, zb = project(xs_ref[...])
        us = u[0:n_sample_rows]
        st = st_ref[...]
        shape3 = (n_sample, sample_seq, tn)
        prev2 = jnp.broadcast_to(st[:, 0:1, :], shape3).reshape(n_sample_rows, tn)
        prev1 = jnp.broadcast_to(st[:, 1:2, :], shape3).reshape(n_sample_rows, tn)
        t = lax.broadcasted_iota(jnp.int32, us.shape, 0) % sample_seq
        os_ref[0:n_sample_rows, :] = _gated_conv(us, zb[0:n_sample_rows], cw_ref, prev1, prev2,
                                                 t).astype(os_ref.dtype)
        sts_ref[...] = us.reshape(shape3)[:, sample_seq - (CONV_WIDTH - 1):sample_seq, :]
        um = u[n_sample_rows:]
        t = lax.broadcasted_iota(jnp.int32, um.shape, 0) % N_META
        zero = jnp.zeros((), F32)
        os_ref[n_sample_rows:, :] = _gated_conv(um, zb[n_sample_rows:], cw_ref, zero, zero,
                                                t).astype(os_ref.dtype)
        for bb in range(n_prompt):
            meta_tail_ref[bb] = um[(bb + 1) * N_META - SUBLANES:(bb + 1) * N_META]

    @pl.when(t_step > 0)
    def _():
        tile = t_step - 1
        bb = tile // tiles_per_seq
        first = tile % tiles_per_seq == 0
        u, zb = project(xm_ref[...])
        rows = u.shape[0]
        tail = jnp.where(first, meta_tail_ref[bb], tail_ref[...])
        prev1 = tail[SUBLANES - 1:SUBLANES, :]
        prev2 = tail[SUBLANES - 2:SUBLANES - 1, :]
        t = lax.broadcasted_iota(jnp.int32, u.shape, 0)
        om_ref[...] = _gated_conv(u, zb, cw_ref, prev1, prev2, t).astype(om_ref.dtype)
        tail_ref[...] = u[rows - SUBLANES:rows]

        @pl.when(tile % tiles_per_seq == tiles_per_seq - 1)
        def _():
            stp_ref[0] = u[rows - (CONV_WIDTH - 1):rows]


def _conv_mixer(xm, xs, w_all, cw_all, layer, state_all, cast_jobs, n_prompt, seq, n_sample, sample_seq):
    rm, k = xm.shape
    rs = xs.shape[0]
    e = CONV_E
    tn, tm = CONV_TN, MAIN_TILE
    nt = e // tn
    n_main = rm // tm
    tiles_per_seq = seq // tm
    ns = CONV_WIDTH - 1

    def main_tile(t):
        return jnp.maximum(t - 1, 0)

    w_specs = [pl.BlockSpec((None, k, tn), functools.partial(lambda j, t, q: (layer, 0, q * nt + j), q=q))
               for q in range(4)]
    cast_in_specs, cast_out_specs, cast_shapes, cast_args = [], [], [], []
    for n, (w_src, src_layer) in enumerate(cast_jobs):
        _, wk, wn = w_src.shape
        rows = wk // (nt * (n + 1))

        def blk(j, t, n=n):
            return j * (n + 1) + jnp.minimum(t, n)

        cast_in_specs.append(pl.BlockSpec((None, rows, wn),
                                          functools.partial(lambda j, t, f, l: (l, f(j, t), 0), f=blk, l=src_layer)))
        cast_out_specs.append(pl.BlockSpec((rows, wn), functools.partial(lambda j, t, f: (f(j, t), 0), f=blk)))
        cast_shapes.append(jax.ShapeDtypeStruct((wk, wn), BF16))
        cast_args.append(w_src)
    kern = functools.partial(_conv_kernel, n_cast=len(cast_jobs), n_sample=n_sample, sample_seq=sample_seq,
                             n_prompt=n_prompt, tiles_per_seq=tiles_per_seq)
    return pl.pallas_call(
        kern, grid=(nt, n_main + 1),
        in_specs=[pl.BlockSpec((tm, k), lambda j, t: (main_tile(t), 0)),
                  pl.BlockSpec((rs, k), lambda j, t: (0, 0))] + w_specs + [
                  pl.BlockSpec((None, CONV_WIDTH, tn), lambda j, t: (layer, 0, j)),
                  pl.BlockSpec((None, n_sample, ns, tn), lambda j, t: (layer, 0, 0, j))] + cast_in_specs,
        out_specs=[pl.BlockSpec((tm, tn), lambda j, t: (main_tile(t), j)),
                   pl.BlockSpec((rs, tn), lambda j, t: (0, j)),
                   pl.BlockSpec((1, ns, tn), lambda j, t: (main_tile(t) // tiles_per_seq, 0, j)),
                   pl.BlockSpec((n_sample, ns, tn), lambda j, t: (0, 0, j))] + cast_out_specs,
        out_shape=[jax.ShapeDtypeStruct((rm, e), BF16),
                   jax.ShapeDtypeStruct((rs, e), BF16),
                   jax.ShapeDtypeStruct((n_prompt, ns, e), F32),
                   jax.ShapeDtypeStruct((n_sample, ns, e), F32)] + cast_shapes,
        scratch_shapes=[pltpu.VMEM((4, k, tn), BF16),
                        pltpu.VMEM((n_prompt, SUBLANES, tn), F32),
                        pltpu.VMEM((SUBLANES, tn), F32)],
        compiler_params=_params("arbitrary", "arbitrary"),
        name="conv_mixer")(xm, xs, w_all, w_all, w_all, w_all, cw_all, state_all, *cast_args)


def _rope_tables(pos):
    freqs = ROPE_BASE ** (-jnp.arange(HALF, dtype=F32) * 2.0 / RET_DK)
    ang = pos.astype(F32)[:, None] * freqs[None, :]
    return jnp.cos(ang), jnp.sin(ang)


def _decay_tables(length):
    lg = jnp.log1p(-jnp.exp2(-5.0 - jnp.arange(RET_HEADS, dtype=F32)))
    idx = jnp.arange(length, dtype=F32)
    diff = idx[:, None] - idx[None, :]
    dmask = jnp.exp(jnp.where(diff[None] >= 0, diff[None] * lg[:, None, None], -jnp.inf))
    q_decay = jnp.exp((idx + 1.0)[None, :] * lg[:, None])[:, :, None]
    k_decay = jnp.exp((length - 1.0 - idx)[None, :] * lg[:, None])[:, :, None]
    s_decay = jnp.exp(length * lg)
    return dmask, q_decay, k_decay, s_decay


def kernel(x_prompt, x_sample, state_ret, state_conv, meta, w_ret_in, w_ret_out, w_conv_in, conv_w,
           w_conv_out, ln_g, ln_b):
    bp, lp, d = x_prompt.shape
    bs, ls, _ = x_sample.shape
    n_sample_rows = bs * ls

    xm = x_prompt.reshape(bp * lp, d)
    xs = jnp.concatenate([x_sample.reshape(n_sample_rows, d),
                          jnp.tile(meta.astype(x_prompt.dtype), (bp, 1))], axis=0)
    xm_bf, xs_bf = xm, xs.astype(BF16)

    rope_m = _rope_tables(N_META + jnp.arange(lp, dtype=jnp.int32))
    pos_s = jnp.concatenate([jnp.tile(PAST_LEN + jnp.arange(ls, dtype=jnp.int32), bs),
                             jnp.tile(jnp.arange(N_META, dtype=jnp.int32), bp)])
    rope_s = _rope_tables(pos_s)

    dm_p, qd_p, kd_p, sdec_p = _decay_tables(PROMPT_CHUNK)
    _, _, kd_m, _ = _decay_tables(N_META)
    dm_s, qd_s, kd_s, sdec_s = _decay_tables(ls)
    tabs = dict(dm_p=dm_p, qd_p=qd_p, kd_p=kd_p, sdec_p=sdec_p, kd_m=kd_m,
                dm_s=dm_s, qd_s=qd_s, kd_s=kd_s, sdec_s=sdec_s)

    w_ret_out_next = w_ret_out[0].astype(BF16)

    ret_p = ret_s = None
    conv_p, conv_s = [], []
    for i in range(DEPTH):
        j = i // 2
        g_i, b_i = ln_g[i][None, :], ln_b[i][None, :]
        if i % 2 == 0:
            pm, ps = _ret_proj(xm_bf, xs_bf, w_ret_in, j, rope_m, rope_s, lp)
            am, a_meta, a_sample, ret_p, ret_s = _ret_mixer(pm, ps, state_ret, j, ret_p, ret_s, tabs,
                                                            bp, lp, bs, ls)
            a_small = jnp.concatenate([a_sample, a_meta], axis=0)
            w_out = w_ret_out_next
        else:
            jobs = [(w_conv_out, j)]
            if j + 1 < w_ret_out.shape[0]:
                jobs.append((w_ret_out, j + 1))
            am, a_small, sp, ss, w_out, *nxt = _conv_mixer(xm_bf, xs_bf, w_conv_in, conv_w, j, state_conv,
                                                           jobs, bp, lp, bs, ls)
            if nxt:
                w_ret_out_next = nxt[0]
            conv_p.append(sp)
            conv_s.append(ss)
        xm, xm_bf = _out_ln(am, w_out, xm, g_i, b_i, OUT_TM, OUT_SUB)
        if i + 1 < DEPTH:
            half = xs.shape[0] // 2
            xs, xs_bf = _out_ln(a_small, w_out, xs, g_i, b_i, half, (half // 2, half // 2))
        else:
            xs, _ = _out_ln(a_small, w_out, xs, g_i, b_i, OUT_TM, OUT_SUB, rows=n_sample_rows)

    y_prompt = xm.reshape(bp, lp, d)
    y_sample = xs.reshape(bs, ls, d)
    return (y_prompt, y_sample, ret_p, ret_s, jnp.stack(conv_p), jnp.stack(conv_s))
```

```python
import functools

import jax
import jax.numpy as jnp
from jax import lax
from jax.experimental import pallas as pl
from jax.experimental.pallas import tpu as pltpu

D_MODEL = 2048
DEPTH = 4
PAST_LEN = 16384
N_META = 16
RET_HEADS = 8
RET_DK = D_MODEL // RET_HEADS
RET_DV = 2 * RET_DK
RET_QK = RET_HEADS * RET_DK
RET_V = RET_HEADS * RET_DV
CONV_WIDTH = 3
CONV_E = D_MODEL
ROPE_BASE = 10000.0
LN_EPS = 1e-5
GN_EPS = 1e-6
DEEPNORM_ALPHA = (2.0 * DEPTH) ** 0.25

HALF = RET_DK // 2
SUBLANES = 8
PROMPT_CHUNK = 256
SAMPLE_GROUP = 2
RET_HEAD_GROUP = 2
RET_PARTS = 4
MAIN_TILE = 1024
PROJ_TN = 1024
CONV_TN = 256
PROJ_SUBBLOCKS = 4
PROJ_SUB_ROWS = 256
OUT_TM = 512
OUT_SUB = (256, 256)
STATE_SLOTS = 3
VMEM_LIMIT = 56 * 1024 * 1024
VMEM_LIMIT_HIGH = 60 * 1024 * 1024

BF16 = jnp.bfloat16
F32 = jnp.float32


def _params(*sem):
    return pltpu.CompilerParams(dimension_semantics=sem, vmem_limit_bytes=VMEM_LIMIT)


def _dot(a, b):
    return jnp.dot(a, b, preferred_element_type=F32)


def _dot_nt(a, b):
    return lax.dot_general(a, b, (((1,), (1,)), ((), ())), preferred_element_type=F32)


def _dot_tn(a, b):
    return lax.dot_general(a, b, (((0,), (0,)), ((), ())), preferred_element_type=F32)


def _silu(x):
    return x * (1.0 / (1.0 + jnp.exp(-x)))


def _proj_kernel(xm_ref, xs_ref, w_ref, cm_ref, sm_ref, cs_ref, ss_ref, om_ref, os_ref, wbf_ref, *,
                 n_main, tn):
    j = pl.program_id(0)
    i = pl.program_id(1)
    rope_tiles = 2 * RET_QK // tn
    q_tiles = RET_QK // tn

    @pl.when(i == 0)
    def _():
        wbf_ref[...] = w_ref[...].astype(BF16)

    is_rope = j < rope_tiles
    scale = jnp.where(j < q_tiles, RET_DK ** -0.5, 1.0).astype(F32)

    def path(x_ref, c_ref, s_ref, o_ref):
        n_sub = max(PROJ_SUBBLOCKS, x_ref.shape[0] // PROJ_SUB_ROWS)
        sub = x_ref.shape[0] // n_sub
        for r in range(n_sub):
            rows = slice(r * sub, (r + 1) * sub)
            acc = _dot(x_ref[rows, :].astype(BF16), wbf_ref[...])
            c = c_ref[rows, :] * scale
            s = s_ref[rows, :] * scale
            for h in range(tn // RET_DK):
                lo = slice(h * RET_DK, h * RET_DK + HALF)
                hi = slice(h * RET_DK + HALF, (h + 1) * RET_DK)
                t1 = acc[:, lo]
                t2 = acc[:, hi]
                o_ref[rows, lo] = jnp.where(is_rope, t1 * c - t2 * s, t1).astype(o_ref.dtype)
                o_ref[rows, hi] = jnp.where(is_rope, t1 * s + t2 * c, t2).astype(o_ref.dtype)

    @pl.when(i < n_main)
    def _():
        path(xm_ref, cm_ref, sm_ref, om_ref)

    @pl.when(i == n_main)
    def _():
        path(xs_ref, cs_ref, ss_ref, os_ref)


def _ret_proj(xm, xs, w_all, layer, rope_m, rope_s, seq):
    rm, k = xm.shape
    rs = xs.shape[0]
    n = w_all.shape[2]
    tn = PROJ_TN
    tm = MAIN_TILE if xm.dtype == F32 else 2 * MAIN_TILE
    n_main = rm // tm
    tiles_per_seq = seq // tm
    last = n_main - 1
    in_specs = [pl.BlockSpec((tm, k), lambda j, i: (jnp.minimum(i, last), 0)),
                pl.BlockSpec((rs, k), lambda j, i: (0, 0)),
                pl.BlockSpec((None, k, tn), lambda j, i: (layer, 0, j)),
                pl.BlockSpec((tm, HALF), lambda j, i: (jnp.minimum(i, last) % tiles_per_seq, 0)),
                pl.BlockSpec((tm, HALF), lambda j, i: (jnp.minimum(i, last) % tiles_per_seq, 0)),
                pl.BlockSpec((rs, HALF), lambda j, i: (0, 0)),
                pl.BlockSpec((rs, HALF), lambda j, i: (0, 0))]
    out_specs = [pl.BlockSpec((tm, tn), lambda j, i: (jnp.minimum(i, last), j)),
                 pl.BlockSpec((rs, tn), lambda j, i: (0, j))]
    out_shape = [jax.ShapeDtypeStruct((rm, n), BF16), jax.ShapeDtypeStruct((rs, n), BF16)]
    args = [xm, xs, w_all, rope_m[0], rope_m[1], rope_s[0], rope_s[1]]
    kern = functools.partial(_proj_kernel, n_main=n_main, tn=tn)
    return pl.pallas_call(
        kern, grid=(n // tn, n_main + 1),
        in_specs=in_specs, out_specs=out_specs, out_shape=out_shape,
        scratch_shapes=[pltpu.VMEM((k, tn), BF16)],
        compiler_params=pltpu.CompilerParams(dimension_semantics=("arbitrary", "arbitrary"),
                                             vmem_limit_bytes=VMEM_LIMIT_HIGH),
        name="ret_proj")(*args)


def _out_ln_kernel(a_ref, w_ref, x_ref, g_ref, b_ref, y_ref, ybf_ref, *, sub):
    r0 = 0
    for n_rows in sub:
        rows = slice(r0, r0 + n_rows)
        r0 += n_rows
        y = DEEPNORM_ALPHA * x_ref[rows, :] + _dot(a_ref[rows, :], w_ref[...])
        mu = jnp.mean(y, axis=-1, keepdims=True)
        d = y - mu
        var = jnp.mean(d * d, axis=-1, keepdims=True)
        out = d * lax.rsqrt(var + LN_EPS) * g_ref[...] + b_ref[...]
        y_ref[rows, :] = out
        ybf_ref[rows, :] = out.astype(BF16)


def _out_ln(a, w, x, g, b, tm, sub, rows=None):
    m, k = a.shape
    m = m if rows is None else rows
    n = w.shape[1]
    kern = functools.partial(_out_ln_kernel, sub=sub)
    return pl.pallas_call(
        kern, grid=(m // tm,),
        in_specs=[pl.BlockSpec((tm, k), lambda i: (i, 0)),
                  pl.BlockSpec((k, n), lambda i: (0, 0), pipeline_mode=pl.Buffered(1)),
                  pl.BlockSpec((tm, n), lambda i: (i, 0)),
                  pl.BlockSpec((1, n), lambda i: (0, 0)),
                  pl.BlockSpec((1, n), lambda i: (0, 0))],
        out_specs=[pl.BlockSpec((tm, n), lambda i: (i, 0)),
                   pl.BlockSpec((tm, n), lambda i: (i, 0))],
        out_shape=[jax.ShapeDtypeStruct((m, n), F32), jax.ShapeDtypeStruct((m, n), BF16)],
        compiler_params=_params("parallel"),
        name="out_ln")(a, w, x, g, b)


def _group_norm_gate(y, g):
    mu = jnp.mean(y, axis=-1, keepdims=True)
    d = y - mu
    var = jnp.mean(d * d, axis=-1, keepdims=True)
    return _silu(g.astype(F32)) * (d * lax.rsqrt(var + GN_EPS))


def _ret_mixer_kernel(sdp_ref, sds_ref, q_ref, k_ref, v_ref, g_ref, qm_ref, km_ref, vm_ref, gm_ref,
                      ps_ref, st_hbm, dm_ref, qd_ref, kd_ref, kdm_ref, dms_ref, qds_ref, kds_ref,
                      *rest, n_chunks, sample_seq, layer):
    o_ref, om_ref, osmp_ref, sp_out_ref, ss_out_ref, s_ref, y_ref, st_buf, st_sem = rest[-9:]
    hg = pl.program_id(0)
    b = pl.program_id(1)
    part = pl.program_id(2)
    c_len = PROMPT_CHUNK

    step = (hg * pl.num_programs(1) + b) * pl.num_programs(2) + part
    n_steps = pl.num_programs(0) * pl.num_programs(1) * pl.num_programs(2)

    def state_copy(s_idx):
        slot = s_idx % STATE_SLOTS
        src = st_hbm.at[layer, pl.ds(s_idx * SAMPLE_GROUP, SAMPLE_GROUP)]
        return pltpu.make_async_copy(src, st_buf.at[slot], st_sem.at[slot])

    @pl.when(step == 0)
    def _():
        for s0 in range(STATE_SLOTS - 1):
            state_copy(s0).start()

    @pl.when(step + (STATE_SLOTS - 1) < n_steps)
    def _():
        state_copy(step + (STATE_SLOTS - 1)).start()

    state_copy(step).wait()
    st_ref = st_buf.at[step % STATE_SLOTS]

    def qk_cols(e):
        return slice(e * RET_DK, (e + 1) * RET_DK)

    def v_cols(e):
        return slice(e * RET_DV, (e + 1) * RET_DV)

    @pl.when(part == 0)
    def _():
        r_meta = pl.multiple_of(b * N_META, N_META)
        for e in range(RET_HEAD_GROUP):
            qm = qm_ref[:, qk_cols(e)]
            km = km_ref[:, qk_cols(e)]
            vm = vm_ref[:, v_cols(e)]
            p = (_dot_nt(qm, km) * dm_ref[e, 0:N_META, 0:N_META]).astype(BF16)
            om_ref[pl.ds(r_meta, N_META), v_cols(e)] = _group_norm_gate(
                _dot(p, vm), gm_ref[:, v_cols(e)]).astype(om_ref.dtype)
            kdk = (km.astype(F32) * kdm_ref[e]).astype(BF16)
            s_ref[e] = _dot_tn(kdk, vm)

    def prompt_chunk(e, c):
        rows = slice(c * c_len, (c + 1) * c_len)
        qc = q_ref[rows, qk_cols(e)]
        kc = k_ref[rows, qk_cols(e)]
        vc = v_ref[rows, v_cols(e)]
        s = s_ref[e]
        p = (_dot_nt(qc, kc) * dm_ref[e]).astype(BF16)
        y = _dot(p, vc) + _dot(qc, s.astype(BF16)) * qd_ref[e]
        o_ref[rows, v_cols(e)] = _group_norm_gate(y, g_ref[rows, v_cols(e)]).astype(o_ref.dtype)
        kdk = (kc.astype(F32) * kd_ref[e]).astype(BF16)
        s_ref[e] = s * sdp_ref[hg * RET_HEAD_GROUP + e] + _dot_tn(kdk, vc)

    pf = ps_ref[...].astype(F32)

    def sample_head(gi, hh):
        rows = slice(gi * sample_seq, (gi + 1) * sample_seq)
        qb = pf[rows, hh * RET_DK:(hh + 1) * RET_DK].astype(BF16)
        k = pf[rows, RET_QK + hh * RET_DK:RET_QK + (hh + 1) * RET_DK]
        vb = pf[rows, 2 * RET_QK + hh * RET_DV:2 * RET_QK + (hh + 1) * RET_DV].astype(BF16)
        s = st_ref[gi, hh]
        p = (_dot_nt(qb, k.astype(BF16)) * dms_ref[hh]).astype(BF16)
        y_ref[rows, hh * RET_DV:(hh + 1) * RET_DV] = (
            _dot(p, vb) + _dot(qb, s.astype(BF16)) * qds_ref[hh])
        kdk = (k * kds_ref[hh]).astype(BF16)
        ss_out_ref[gi, hh] = s * sds_ref[hh] + _dot_tn(kdk, vb)

    pairs = [(gi, hh) for gi in range(SAMPLE_GROUP) for hh in range(RET_HEADS)]
    bodies = [(e, c) for c in range(n_chunks) for e in range(RET_HEAD_GROUP)]
    per_body = len(pairs) // len(bodies)
    for n, (e, c) in enumerate(bodies):
        prompt_chunk(e, c)
        for gi, hh in pairs[n * per_body:(n + 1) * per_body]:
            sample_head(gi, hh)
    for gi, hh in pairs[len(bodies) * per_body:]:
        sample_head(gi, hh)
    g0 = 2 * RET_QK + RET_V
    for hh in range(RET_HEADS):
        cols = slice(hh * RET_DV, (hh + 1) * RET_DV)
        g = pf[:, g0 + hh * RET_DV:g0 + (hh + 1) * RET_DV]
        osmp_ref[:, cols] = _group_norm_gate(y_ref[:, cols], g).astype(osmp_ref.dtype)

    @pl.when(part == pl.num_programs(2) - 1)
    def _():
        sp_out_ref[...] = s_ref[...]


def _ret_mixer(pm, ps, state_all, layer, new_p_all, new_s_all, tabs, batch, seq, n_sample, sample_seq):
    c_len = PROMPT_CHUNK
    hd = RET_HEADS
    hgs = RET_HEAD_GROUP
    n_hg = hd // hgs
    parts = RET_PARTS
    rows = seq // parts
    grows = SAMPLE_GROUP * sample_seq
    assert n_hg * batch * parts * SAMPLE_GROUP == n_sample
    n_layers = state_all.shape[0]
    meta0 = n_sample * sample_seq // N_META
    qkw, vw = hgs * RET_DK, hgs * RET_DV
    k_blocks = RET_QK // qkw
    v_blocks = 2 * RET_QK // vw
    g_blocks = v_blocks + RET_V // vw

    def grp(h, b, p):
        return (h * batch + b) * parts + p

    kern = functools.partial(_ret_mixer_kernel, n_chunks=rows // c_len, sample_seq=sample_seq, layer=layer)
    p_state_spec = pl.BlockSpec((None, None, hgs, RET_DK, RET_DV), lambda h, b, p: (layer, b, h, 0, 0))
    s_state_spec = pl.BlockSpec((None, SAMPLE_GROUP, hd, RET_DK, RET_DV),
                                lambda h, b, p: (layer, grp(h, b, p), 0, 0, 0))
    smem = pl.BlockSpec(memory_space=pltpu.SMEM)
    in_specs = [smem, smem,
                pl.BlockSpec((rows, qkw), lambda h, b, p: (b * parts + p, h)),
                pl.BlockSpec((rows, qkw), lambda h, b, p: (b * parts + p, k_blocks + h)),
                pl.BlockSpec((rows, vw), lambda h, b, p: (b * parts + p, v_blocks + h)),
                pl.BlockSpec((rows, vw), lambda h, b, p: (b * parts + p, g_blocks + h)),
                pl.BlockSpec((N_META, qkw), lambda h, b, p: (meta0 + b, h)),
                pl.BlockSpec((N_META, qkw), lambda h, b, p: (meta0 + b, k_blocks + h)),
                pl.BlockSpec((N_META, vw), lambda h, b, p: (meta0 + b, v_blocks + h)),
                pl.BlockSpec((N_META, vw), lambda h, b, p: (meta0 + b, g_blocks + h)),
                pl.BlockSpec((grows, ps.shape[1]), lambda h, b, p: (grp(h, b, p), 0)),
                pl.BlockSpec(memory_space=pl.ANY),
                pl.BlockSpec((hgs, c_len, c_len), lambda h, b, p: (h, 0, 0)),
                pl.BlockSpec((hgs, c_len, 1), lambda h, b, p: (h, 0, 0)),
                pl.BlockSpec((hgs, c_len, 1), lambda h, b, p: (h, 0, 0)),
                pl.BlockSpec((hgs, N_META, 1), lambda h, b, p: (h, 0, 0)),
                pl.BlockSpec((hd, sample_seq, sample_seq), lambda h, b, p: (0, 0, 0)),
                pl.BlockSpec((hd, sample_seq, 1), lambda h, b, p: (0, 0, 0)),
                pl.BlockSpec((hd, sample_seq, 1), lambda h, b, p: (0, 0, 0))]
    args = [tabs["sdec_p"], tabs["sdec_s"], pm, pm, pm, pm, ps, ps, ps, ps, ps, state_all,
            tabs["dm_p"], tabs["qd_p"], tabs["kd_p"], tabs["kd_m"], tabs["dm_s"], tabs["qd_s"], tabs["kd_s"]]
    aliases = {}
    if new_p_all is not None:
        in_specs += [pl.BlockSpec(memory_space=pl.ANY), pl.BlockSpec(memory_space=pl.ANY)]
        args += [new_p_all, new_s_all]
        aliases = {len(args) - 2: 3, len(args) - 1: 4}
    return pl.pallas_call(
        kern, grid=(n_hg, batch, parts),
        in_specs=in_specs,
        out_specs=[pl.BlockSpec((rows, vw), lambda h, b, p: (b * parts + p, h)),
                   pl.BlockSpec((batch * N_META, vw), lambda h, b, p: (0, h)),
                   pl.BlockSpec((grows, RET_V), lambda h, b, p: (grp(h, b, p), 0)),
                   p_state_spec, s_state_spec],
        out_shape=[jax.ShapeDtypeStruct((batch * seq, RET_V), BF16),
                   jax.ShapeDtypeStruct((batch * N_META, RET_V), BF16),
                   jax.ShapeDtypeStruct((n_sample * sample_seq, RET_V), BF16),
                   jax.ShapeDtypeStruct((n_layers, batch, hd, RET_DK, RET_DV), F32),
                   jax.ShapeDtypeStruct(state_all.shape, F32)],
        scratch_shapes=[pltpu.VMEM((hgs, RET_DK, RET_DV), F32), pltpu.VMEM((grows, RET_V), F32),
                        pltpu.VMEM((STATE_SLOTS, SAMPLE_GROUP, hd, RET_DK, RET_DV), F32),
                        pltpu.SemaphoreType.DMA((STATE_SLOTS,))],
        input_output_aliases=aliases,
        compiler_params=pltpu.CompilerParams(dimension_semantics=("arbitrary",) * 3,
                                             vmem_limit_bytes=VMEM_LIMIT_HIGH),
        name="ret_mixer")(*args)


def _gated_conv(u, zb, cw_ref, prev1, prev2, t):
    u1 = jnp.where(t >= 1, pltpu.roll(u, 1, 0), prev1)
    u2 = jnp.where(t >= 2, pltpu.roll(u, 2, 0), jnp.where(t == 1, prev1, prev2))
    conv = cw_ref[0:1, :] * u2 + cw_ref[1:2, :] * u1 + cw_ref[2:3, :] * u
    return zb * conv


def _conv_kernel(*refs, n_cast, n_sample, sample_seq, n_prompt, tiles_per_seq):
    (xm_ref, xs_ref, wb_ref, wc_ref, wv_ref, wz_ref, cw_ref, st_ref), refs = refs[:8], refs[8:]
    cast_in, refs = refs[:n_cast], refs[n_cast:]
    (om_ref, os_ref, stp_ref, sts_ref), refs = refs[:4], refs[4:]
    cast_out, (wbf_ref, meta_tail_ref, tail_ref) = refs[:n_cast], refs[n_cast:]
    t_step = pl.program_id(1)
    tn = om_ref.shape[1]
    n_sample_rows = n_sample * sample_seq

    for n, (src_ref, dst_ref) in enumerate(zip(cast_in, cast_out)):
        @pl.when(t_step <= n)
        def _(src_ref=src_ref, dst_ref=dst_ref):
            dst_ref[...] = src_ref[...].astype(BF16)

    def project(x):
        u = _dot(x, wbf_ref[1]) * _dot(x, wbf_ref[2])
        zb = _silu(_dot(x, wbf_ref[3])) * _dot(x, wbf_ref[0])
        return u, zb

    @pl.when(t_step == 0)
    def _():
        wbf_ref[0] = wb_ref[...].astype(BF16)
        wbf_ref[1] = wc_ref[...].astype(BF16)
        wbf_ref[2] = wv_ref[...].astype(BF16)
        wbf_ref[3] = wz_ref[...].astype(BF16)
        u, zb = project(xs_ref[...])
        us = u[0:n_sample_rows]
        st = st_ref[...]
        shape3 = (n_sample, sample_seq, tn)
        prev2 = jnp.broadcast_to(st[:, 0:1, :], shape3).reshape(n_sample_rows, tn)
        prev1 = jnp.broadcast_to(st[:, 1:2, :], shape3).reshape(n_sample_rows, tn)
        t = lax.broadcasted_iota(jnp.int32, us.shape, 0) % sample_seq
        os_ref[0:n_sample_rows, :] = _gated_conv(us, zb[0:n_sample_rows], cw_ref, prev1, prev2,
                                                 t).astype(os_ref.dtype)
        sts_ref[...] = us.reshape(shape3)[:, sample_seq - (CONV_WIDTH - 1):sample_seq, :]
        um = u[n_sample_rows:]
        t = lax.broadcasted_iota(jnp.int32, um.shape, 0) % N_META
        zero = jnp.zeros((), F32)
        os_ref[n_sample_rows:, :] = _gated_conv(um, zb[n_sample_rows:], cw_ref, zero, zero,
                                                t).astype(os_ref.dtype)
        for bb in range(n_prompt):
            meta_tail_ref[bb] = um[(bb + 1) * N_META - SUBLANES:(bb + 1) * N_META]

    @pl.when(t_step > 0)
    def _():
        tile = t_step - 1
        bb = tile // tiles_per_seq
        first = tile % tiles_per_seq == 0
        u, zb = project(xm_ref[...])
        rows = u.shape[0]
        tail = jnp.where(first, meta_tail_ref[bb], tail_ref[...])
        prev1 = tail[SUBLANES - 1:SUBLANES, :]
        prev2 = tail[SUBLANES - 2:SUBLANES - 1, :]
        t = lax.broadcasted_iota(jnp.int32, u.shape, 0)
        om_ref[...] = _gated_conv(u, zb, cw_ref, prev1, prev2, t).astype(om_ref.dtype)
        tail_ref[...] = u[rows - SUBLANES:rows]

        @pl.when(tile % tiles_per_seq == tiles_per_seq - 1)
        def _():
            stp_ref[0] = u[rows - (CONV_WIDTH - 1):rows]


def _conv_mixer(xm, xs, w_all, cw_all, layer, state_all, cast_jobs, n_prompt, seq, n_sample, sample_seq):
    rm, k = xm.shape
    rs = xs.shape[0]
    e = CONV_E
    tn, tm = CONV_TN, MAIN_TILE
    nt = e // tn
    n_main = rm // tm
    tiles_per_seq = seq // tm
    ns = CONV_WIDTH - 1

    def main_tile(t):
        return jnp.maximum(t - 1, 0)

    w_specs = [pl.BlockSpec((None, k, tn), functools.partial(lambda j, t, q: (layer, 0, q * nt + j), q=q))
               for q in range(4)]
    cast_in_specs, cast_out_specs, cast_shapes, cast_args = [], [], [], []
    for n, (w_src, src_layer) in enumerate(cast_jobs):
        _, wk, wn = w_src.shape
        rows = wk // (nt * (n + 1))

        def blk(j, t, n=n):
            return j * (n + 1) + jnp.minimum(t, n)

        cast_in_specs.append(pl.BlockSpec((None, rows, wn),
                                          functools.partial(lambda j, t, f, l: (l, f(j, t), 0), f=blk, l=src_layer)))
        cast_out_specs.append(pl.BlockSpec((rows, wn), functools.partial(lambda j, t, f: (f(j, t), 0), f=blk)))
        cast_shapes.append(jax.ShapeDtypeStruct((wk, wn), BF16))
        cast_args.append(w_src)
    kern = functools.partial(_conv_kernel, n_cast=len(cast_jobs), n_sample=n_sample, sample_seq=sample_seq,
                             n_prompt=n_prompt, tiles_per_seq=tiles_per_seq)
    return pl.pallas_call(
        kern, grid=(nt, n_main + 1),
        in_specs=[pl.BlockSpec((tm, k), lambda j, t: (main_tile(t), 0)),
                  pl.BlockSpec((rs, k), lambda j, t: (0, 0))] + w_specs + [
                  pl.BlockSpec((None, CONV_WIDTH, tn), lambda j, t: (layer, 0, j)),
                  pl.BlockSpec((None, n_sample, ns, tn), lambda j, t: (layer, 0, 0, j))] + cast_in_specs,
        out_specs=[pl.BlockSpec((tm, tn), lambda j, t: (main_tile(t), j)),
                   pl.BlockSpec((rs, tn), lambda j, t: (0, j)),
                   pl.BlockSpec((1, ns, tn), lambda j, t: (main_tile(t) // tiles_per_seq, 0, j)),
                   pl.BlockSpec((n_sample, ns, tn), lambda j, t: (0, 0, j))] + cast_out_specs,
        out_shape=[jax.ShapeDtypeStruct((rm, e), BF16),
                   jax.ShapeDtypeStruct((rs, e), BF16),
                   jax.ShapeDtypeStruct((n_prompt, ns, e), F32),
                   jax.ShapeDtypeStruct((n_sample, ns, e), F32)] + cast_shapes,
        scratch_shapes=[pltpu.VMEM((4, k, tn), BF16),
                        pltpu.VMEM((n_prompt, SUBLANES, tn), F32),
                        pltpu.VMEM((SUBLANES, tn), F32)],
        compiler_params=_params("arbitrary", "arbitrary"),
        name="conv_mixer")(xm, xs, w_all, w_all, w_all, w_all, cw_all, state_all, *cast_args)


def _rope_tables(pos):
    freqs = ROPE_BASE ** (-jnp.arange(HALF, dtype=F32) * 2.0 / RET_DK)
    ang = pos.astype(F32)[:, None] * freqs[None, :]
    return jnp.cos(ang), jnp.sin(ang)


def _decay_tables(length):
    lg = jnp.log1p(-jnp.exp2(-5.0 - jnp.arange(RET_HEADS, dtype=F32)))
    idx = jnp.arange(length, dtype=F32)
    diff = idx[:, None] - idx[None, :]
    dmask = jnp.exp(jnp.where(diff[None] >= 0, diff[None] * lg[:, None, None], -jnp.inf))
    q_decay = jnp.exp((idx + 1.0)[None, :] * lg[:, None])[:, :, None]
    k_decay = jnp.exp((length - 1.0 - idx)[None, :] * lg[:, None])[:, :, None]
    s_decay = jnp.exp(length * lg)
    return dmask, q_decay, k_decay, s_decay


def kernel(x_prompt, x_sample, state_ret, state_conv, meta, w_ret_in, w_ret_out, w_conv_in, conv_w,
           w_conv_out, ln_g, ln_b):
    bp, lp, d = x_prompt.shape
    bs, ls, _ = x_sample.shape
    n_sample_rows = bs * ls

    xm = x_prompt.reshape(bp * lp, d)
    xs = jnp.concatenate([x_sample.reshape(n_sample_rows, d),
                          jnp.tile(meta.astype(x_prompt.dtype), (bp, 1))], axis=0)
    xm_bf, xs_bf = xm, xs.astype(BF16)

    rope_m = _rope_tables(N_META + jnp.arange(lp, dtype=jnp.int32))
    pos_s = jnp.concatenate([jnp.tile(PAST_LEN + jnp.arange(ls, dtype=jnp.int32), bs),
                             jnp.tile(jnp.arange(N_META, dtype=jnp.int32), bp)])
    rope_s = _rope_tables(pos_s)

    dm_p, qd_p, kd_p, sdec_p = _decay_tables(PROMPT_CHUNK)
    _, _, kd_m, _ = _decay_tables(N_META)
    dm_s, qd_s, kd_s, sdec_s = _decay_tables(ls)
    tabs = dict(dm_p=dm_p, qd_p=qd_p, kd_p=kd_p, sdec_p=sdec_p, kd_m=kd_m,
                dm_s=dm_s, qd_s=qd_s, kd_s=kd_s, sdec_s=sdec_s)

    w_ret_out_next = w_ret_out[0].astype(BF16)

    ret_p = ret_s = None
    conv_p, conv_s = [], []
    for i in range(DEPTH):
        j = i // 2
        g_i, b_i = ln_g[i][None, :], ln_b[i][None, :]
        if i % 2 == 0:
            pm, ps = _ret_proj(xm_bf, xs_bf, w_ret_in, j, rope_m, rope_s, lp)
            am, a_meta, a_sample, ret_p, ret_s = _ret_mixer(pm, ps, state_ret, j, ret_p, ret_s, tabs,
                                                            bp, lp, bs, ls)
            a_small = jnp.concatenate([a_sample, a_meta], axis=0)
            w_out = w_ret_out_next
        else:
            jobs = [(w_conv_out, j)]
            if j + 1 < w_ret_out.shape[0]:
                jobs.append((w_ret_out, j + 1))
            am, a_small, sp, ss, w_out, *nxt = _conv_mixer(xm_bf, xs_bf, w_conv_in, conv_w, j, state_conv,
                                                           jobs, bp, lp, bs, ls)
            if nxt:
                w_ret_out_next = nxt[0]
            conv_p.append(sp)
            conv_s.append(ss)
        xm, xm_bf = _out_ln(am, w_out, xm, g_i, b_i, OUT_TM, OUT_SUB)
        if i + 1 < DEPTH:
            half = xs.shape[0] // 2
            xs, xs_bf = _out_ln(a_small, w_out, xs, g_i, b_i, half, (half // 2, half // 2))
        else:
            xs, _ = _out_ln(a_small, w_out, xs, g_i, b_i, OUT_TM, OUT_SUB, rows=n_sample_rows)

    y_prompt = xm.reshape(bp, lp, d)
    y_sample = xs.reshape(bs, ls, d)
    return (y_prompt, y_sample, ret_p, ret_s, jnp.stack(conv_p), jnp.stack(conv_s))
```

```python
import functools

import jax
import jax.numpy as jnp
from jax import lax
from jax.experimental import pallas as pl
from jax.experimental.pallas import tpu as pltpu

D_MODEL = 2048
DEPTH = 4
PAST_LEN = 16384
N_META = 16
RET_HEADS = 8
RET_DK = D_MODEL // RET_HEADS
RET_DV = 2 * RET_DK
RET_QK = RET_HEADS * RET_DK
RET_V = RET_HEADS * RET_DV
CONV_WIDTH = 3
CONV_E = D_MODEL
ROPE_BASE = 10000.0
LN_EPS = 1e-5
GN_EPS = 1e-6
DEEPNORM_ALPHA = (2.0 * DEPTH) ** 0.25

HALF = RET_DK // 2
SUBLANES = 8
PROMPT_CHUNK = 256
SAMPLE_GROUP = 2
RET_HEAD_GROUP = 2
RET_PARTS = 4
MAIN_TILE = 1024
PROJ_TN = 1024
CONV_TN = 256
PROJ_SUBBLOCKS = 4
PROJ_SUB_ROWS = 256
OUT_TM = 512
OUT_SUB = (256, 256)
STATE_SLOTS = 3
VMEM_LIMIT = 56 * 1024 * 1024
VMEM_LIMIT_HIGH = 60 * 1024 * 1024

BF16 = jnp.bfloat16
F32 = jnp.float32


def _params(*sem):
    return pltpu.CompilerParams(dimension_semantics=sem, vmem_limit_bytes=VMEM_LIMIT)


def _dot(a, b):
    return jnp.dot(a, b, preferred_element_type=F32)


def _dot_nt(a, b):
    return lax.dot_general(a, b, (((1,), (1,)), ((), ())), preferred_element_type=F32)


def _dot_tn(a, b):
    return lax.dot_general(a, b, (((0,), (0,)), ((), ())), preferred_element_type=F32)


def _silu(x):
    return x * (1.0 / (1.0 + jnp.exp(-x)))


def _proj_kernel(xm_ref, xs_ref, w_ref, cm_ref, sm_ref, cs_ref, ss_ref, om_ref, os_ref, wbf_ref, *,
                 n_main, tn):
    j = pl.program_id(0)
    i = pl.program_id(1)
    rope_tiles = 2 * RET_QK // tn
    q_tiles = RET_QK // tn

    @pl.when(i == 0)
    def _():
        wbf_ref[...] = w_ref[...].astype(BF16)

    is_rope = j < rope_tiles
    scale = jnp.where(j < q_tiles, RET_DK ** -0.5, 1.0).astype(F32)

    def path(x_ref, c_ref, s_ref, o_ref, rope):
        n_sub = max(PROJ_SUBBLOCKS, x_ref.shape[0] // PROJ_SUB_ROWS)
        sub = x_ref.shape[0] // n_sub
        for r in range(n_sub):
            rows = slice(r * sub, (r + 1) * sub)
            acc = _dot(x_ref[rows, :].astype(BF16), wbf_ref[...])
            if not rope:
                o_ref[rows, :] = acc.astype(o_ref.dtype)
                continue
            c = c_ref[rows, :] * scale
            s = s_ref[rows, :] * scale
            for h in range(tn // RET_DK):
                lo = slice(h * RET_DK, h * RET_DK + HALF)
                hi = slice(h * RET_DK + HALF, (h + 1) * RET_DK)
                t1 = acc[:, lo]
                t2 = acc[:, hi]
                o_ref[rows, lo] = (t1 * c - t2 * s).astype(o_ref.dtype)
                o_ref[rows, hi] = (t1 * s + t2 * c).astype(o_ref.dtype)

    for rope in (True, False):
        tile_kind = is_rope if rope else jnp.logical_not(is_rope)

        @pl.when(tile_kind & (i < n_main))
        def _(rope=rope):
            path(xm_ref, cm_ref, sm_ref, om_ref, rope)

        @pl.when(tile_kind & (i == n_main))
        def _(rope=rope):
            path(xs_ref, cs_ref, ss_ref, os_ref, rope)


def _ret_proj(xm, xs, w_all, layer, rope_m, rope_s, seq):
    rm, k = xm.shape
    rs = xs.shape[0]
    n = w_all.shape[2]
    tn = PROJ_TN
    tm = MAIN_TILE if xm.dtype == F32 else 2 * MAIN_TILE
    n_main = rm // tm
    tiles_per_seq = seq // tm
    last = n_main - 1
    in_specs = [pl.BlockSpec((tm, k), lambda j, i: (jnp.minimum(i, last), 0)),
                pl.BlockSpec((rs, k), lambda j, i: (0, 0)),
                pl.BlockSpec((None, k, tn), lambda j, i: (layer, 0, j)),
                pl.BlockSpec((tm, HALF), lambda j, i: (jnp.minimum(i, last) % tiles_per_seq, 0)),
                pl.BlockSpec((tm, HALF), lambda j, i: (jnp.minimum(i, last) % tiles_per_seq, 0)),
                pl.BlockSpec((rs, HALF), lambda j, i: (0, 0)),
                pl.BlockSpec((rs, HALF), lambda j, i: (0, 0))]
    out_specs = [pl.BlockSpec((tm, tn), lambda j, i: (jnp.minimum(i, last), j)),
                 pl.BlockSpec((rs, tn), lambda j, i: (0, j))]
    out_shape = [jax.ShapeDtypeStruct((rm, n), BF16), jax.ShapeDtypeStruct((rs, n), BF16)]
    args = [xm, xs, w_all, rope_m[0], rope_m[1], rope_s[0], rope_s[1]]
    kern = functools.partial(_proj_kernel, n_main=n_main, tn=tn)
    return pl.pallas_call(
        kern, grid=(n // tn, n_main + 1),
        in_specs=in_specs, out_specs=out_specs, out_shape=out_shape,
        scratch_shapes=[pltpu.VMEM((k, tn), BF16)],
        compiler_params=pltpu.CompilerParams(dimension_semantics=("arbitrary", "arbitrary"),
                                             vmem_limit_bytes=VMEM_LIMIT_HIGH),
        name="ret_proj")(*args)


def _out_ln_kernel(a_ref, w_ref, x_ref, g_ref, b_ref, y_ref, ybf_ref, *, sub):
    r0 = 0
    for n_rows in sub:
        rows = slice(r0, r0 + n_rows)
        r0 += n_rows
        y = DEEPNORM_ALPHA * x_ref[rows, :] + _dot(a_ref[rows, :], w_ref[...])
        mu = jnp.mean(y, axis=-1, keepdims=True)
        d = y - mu
        var = jnp.mean(d * d, axis=-1, keepdims=True)
        out = d * lax.rsqrt(var + LN_EPS) * g_ref[...] + b_ref[...]
        y_ref[rows, :] = out
        ybf_ref[rows, :] = out.astype(BF16)


def _out_ln(a, w, x, g, b, tm, sub, rows=None):
    m, k = a.shape
    m = m if rows is None else rows
    n = w.shape[1]
    kern = functools.partial(_out_ln_kernel, sub=sub)
    return pl.pallas_call(
        kern, grid=(m // tm,),
        in_specs=[pl.BlockSpec((tm, k), lambda i: (i, 0)),
                  pl.BlockSpec((k, n), lambda i: (0, 0), pipeline_mode=pl.Buffered(1)),
                  pl.BlockSpec((tm, n), lambda i: (i, 0)),
                  pl.BlockSpec((1, n), lambda i: (0, 0)),
                  pl.BlockSpec((1, n), lambda i: (0, 0))],
        out_specs=[pl.BlockSpec((tm, n), lambda i: (i, 0)),
                   pl.BlockSpec((tm, n), lambda i: (i, 0))],
        out_shape=[jax.ShapeDtypeStruct((m, n), F32), jax.ShapeDtypeStruct((m, n), BF16)],
        compiler_params=_params("parallel"),
        name="out_ln")(a, w, x, g, b)


def _group_norm_gate(y, g):
    mu = jnp.mean(y, axis=-1, keepdims=True)
    d = y - mu
    var = jnp.mean(d * d, axis=-1, keepdims=True)
    return _silu(g.astype(F32)) * (d * lax.rsqrt(var + GN_EPS))


def _ret_mixer_kernel(sdp_ref, sds_ref, q_ref, k_ref, v_ref, g_ref, qm_ref, km_ref, vm_ref, gm_ref,
                      ps_ref, st_hbm, dm_ref, qd_ref, kd_ref, kdm_ref, dms_ref, qds_ref, kds_ref,
                      *rest, n_chunks, sample_seq, layer):
    o_ref, om_ref, osmp_ref, sp_out_ref, ss_out_ref, s_ref, y_ref, st_buf, st_sem = rest[-9:]
    hg = pl.program_id(0)
    b = pl.program_id(1)
    part = pl.program_id(2)
    c_len = PROMPT_CHUNK

    step = (hg * pl.num_programs(1) + b) * pl.num_programs(2) + part
    n_steps = pl.num_programs(0) * pl.num_programs(1) * pl.num_programs(2)

    def state_copy(s_idx):
        slot = s_idx % STATE_SLOTS
        src = st_hbm.at[layer, pl.ds(s_idx * SAMPLE_GROUP, SAMPLE_GROUP)]
        return pltpu.make_async_copy(src, st_buf.at[slot], st_sem.at[slot])

    @pl.when(step == 0)
    def _():
        for s0 in range(STATE_SLOTS - 1):
            state_copy(s0).start()

    @pl.when(step + (STATE_SLOTS - 1) < n_steps)
    def _():
        state_copy(step + (STATE_SLOTS - 1)).start()

    state_copy(step).wait()
    st_ref = st_buf.at[step % STATE_SLOTS]

    def qk_cols(e):
        return slice(e * RET_DK, (e + 1) * RET_DK)

    def v_cols(e):
        return slice(e * RET_DV, (e + 1) * RET_DV)

    @pl.when(part == 0)
    def _():
        r_meta = pl.multiple_of(b * N_META, N_META)
        for e in range(RET_HEAD_GROUP):
            qm = qm_ref[:, qk_cols(e)]
            km = km_ref[:, qk_cols(e)]
            vm = vm_ref[:, v_cols(e)]
            p = (_dot_nt(qm, km) * dm_ref[e, 0:N_META, 0:N_META]).astype(BF16)
            om_ref[pl.ds(r_meta, N_META), v_cols(e)] = _group_norm_gate(
                _dot(p, vm), gm_ref[:, v_cols(e)]).astype(om_ref.dtype)
            kdk = (km.astype(F32) * kdm_ref[e]).astype(BF16)
            s_ref[e] = _dot_tn(kdk, vm)

    def prompt_chunk(e, c):
        rows = slice(c * c_len, (c + 1) * c_len)
        qc = q_ref[rows, qk_cols(e)]
        kc = k_ref[rows, qk_cols(e)]
        vc = v_ref[rows, v_cols(e)]
        s = s_ref[e]
        p = (_dot_nt(qc, kc) * dm_ref[e]).astype(BF16)
        y = _dot(p, vc) + _dot(qc, s.astype(BF16)) * qd_ref[e]
        o_ref[rows, v_cols(e)] = _group_norm_gate(y, g_ref[rows, v_cols(e)]).astype(o_ref.dtype)
        kdk = (kc.astype(F32) * kd_ref[e]).astype(BF16)
        s_ref[e] = s * sdp_ref[hg * RET_HEAD_GROUP + e] + _dot_tn(kdk, vc)

    pf = ps_ref[...].astype(F32)

    def sample_head(gi, hh):
        rows = slice(gi * sample_seq, (gi + 1) * sample_seq)
        qb = pf[rows, hh * RET_DK:(hh + 1) * RET_DK].astype(BF16)
        k = pf[rows, RET_QK + hh * RET_DK:RET_QK + (hh + 1) * RET_DK]
        vb = pf[rows, 2 * RET_QK + hh * RET_DV:2 * RET_QK + (hh + 1) * RET_DV].astype(BF16)
        s = st_ref[gi, hh]
        p = (_dot_nt(qb, k.astype(BF16)) * dms_ref[hh]).astype(BF16)
        y_ref[rows, hh * RET_DV:(hh + 1) * RET_DV] = (
            _dot(p, vb) + _dot(qb, s.astype(BF16)) * qds_ref[hh])
        kdk = (k * kds_ref[hh]).astype(BF16)
        ss_out_ref[gi, hh] = s * sds_ref[hh] + _dot_tn(kdk, vb)

    pairs = [(gi, hh) for gi in range(SAMPLE_GROUP) for hh in range(RET_HEADS)]
    bodies = [(e, c) for c in range(n_chunks) for e in range(RET_HEAD_GROUP)]
    per_body = len(pairs) // len(bodies)
    for n, (e, c) in enumerate(bodies):
        prompt_chunk(e, c)
        for gi, hh in pairs[n * per_body:(n + 1) * per_body]:
            sample_head(gi, hh)
    for gi, hh in pairs[len(bodies) * per_body:]:
        sample_head(gi, hh)
    g0 = 2 * RET_QK + RET_V
    for hh in range(RET_HEADS):
        cols = slice(hh * RET_DV, (hh + 1) * RET_DV)
        g = pf[:, g0 + hh * RET_DV:g0 + (hh + 1) * RET_DV]
        osmp_ref[:, cols] = _group_norm_gate(y_ref[:, cols], g).astype(osmp_ref.dtype)

    @pl.when(part == pl.num_programs(2) - 1)
    def _():
        sp_out_ref[...] = s_ref[...]


def _ret_mixer(pm, ps, state_all, layer, new_p_all, new_s_all, tabs, batch, seq, n_sample, sample_seq):
    c_len = PROMPT_CHUNK
    hd = RET_HEADS
    hgs = RET_HEAD_GROUP
    n_hg = hd // hgs
    parts = RET_PARTS
    rows = seq // parts
    grows = SAMPLE_GROUP * sample_seq
    assert n_hg * batch * parts * SAMPLE_GROUP == n_sample
    n_layers = state_all.shape[0]
    meta0 = n_sample * sample_seq // N_META
    qkw, vw = hgs * RET_DK, hgs * RET_DV
    k_blocks = RET_QK // qkw
    v_blocks = 2 * RET_QK // vw
    g_blocks = v_blocks + RET_V // vw

    def grp(h, b, p):
        return (h * batch + b) * parts + p

    kern = functools.partial(_ret_mixer_kernel, n_chunks=rows // c_len, sample_seq=sample_seq, layer=layer)
    p_state_spec = pl.BlockSpec((None, None, hgs, RET_DK, RET_DV), lambda h, b, p: (layer, b, h, 0, 0))
    s_state_spec = pl.BlockSpec((None, SAMPLE_GROUP, hd, RET_DK, RET_DV),
                                lambda h, b, p: (layer, grp(h, b, p), 0, 0, 0))
    smem = pl.BlockSpec(memory_space=pltpu.SMEM)
    in_specs = [smem, smem,
                pl.BlockSpec((rows, qkw), lambda h, b, p: (b * parts + p, h)),
                pl.BlockSpec((rows, qkw), lambda h, b, p: (b * parts + p, k_blocks + h)),
                pl.BlockSpec((rows, vw), lambda h, b, p: (b * parts + p, v_blocks + h)),
                pl.BlockSpec((rows, vw), lambda h, b, p: (b * parts + p, g_blocks + h)),
                pl.BlockSpec((N_META, qkw), lambda h, b, p: (meta0 + b, h)),
                pl.BlockSpec((N_META, qkw), lambda h, b, p: (meta0 + b, k_blocks + h)),
                pl.BlockSpec((N_META, vw), lambda h, b, p: (meta0 + b, v_blocks + h)),
                pl.BlockSpec((N_META, vw), lambda h, b, p: (meta0 + b, g_blocks + h)),
                pl.BlockSpec((grows, ps.shape[1]), lambda h, b, p: (grp(h, b, p), 0)),
                pl.BlockSpec(memory_space=pl.ANY),
                pl.BlockSpec((hgs, c_len, c_len), lambda h, b, p: (h, 0, 0)),
                pl.BlockSpec((hgs, c_len, 1), lambda h, b, p: (h, 0, 0)),
                pl.BlockSpec((hgs, c_len, 1), lambda h, b, p: (h, 0, 0)),
                pl.BlockSpec((hgs, N_META, 1), lambda h, b, p: (h, 0, 0)),
                pl.BlockSpec((hd, sample_seq, sample_seq), lambda h, b, p: (0, 0, 0)),
                pl.BlockSpec((hd, sample_seq, 1), lambda h, b, p: (0, 0, 0)),
                pl.BlockSpec((hd, sample_seq, 1), lambda h, b, p: (0, 0, 0))]
    args = [tabs["sdec_p"], tabs["sdec_s"], pm, pm, pm, pm, ps, ps, ps, ps, ps, state_all,
            tabs["dm_p"], tabs["qd_p"], tabs["kd_p"], tabs["kd_m"], tabs["dm_s"], tabs["qd_s"], tabs["kd_s"]]
    aliases = {}
    if new_p_all is not None:
        in_specs += [pl.BlockSpec(memory_space=pl.ANY), pl.BlockSpec(memory_space=pl.ANY)]
        args += [new_p_all, new_s_all]
        aliases = {len(args) - 2: 3, len(args) - 1: 4}
    return pl.pallas_call(
        kern, grid=(n_hg, batch, parts),
        in_specs=in_specs,
        out_specs=[pl.BlockSpec((rows, vw), lambda h, b, p: (b * parts + p, h)),
                   pl.BlockSpec((batch * N_META, vw), lambda h, b, p: (0, h)),
                   pl.BlockSpec((grows, RET_V), lambda h, b, p: (grp(h, b, p), 0)),
                   p_state_spec, s_state_spec],
        out_shape=[jax.ShapeDtypeStruct((batch * seq, RET_V), BF16),
                   jax.ShapeDtypeStruct((batch * N_META, RET_V), BF16),
                   jax.ShapeDtypeStruct((n_sample * sample_seq, RET_V), BF16),
                   jax.ShapeDtypeStruct((n_layers, batch, hd, RET_DK, RET_DV), F32),
                   jax.ShapeDtypeStruct(state_all.shape, F32)],
        scratch_shapes=[pltpu.VMEM((hgs, RET_DK, RET_DV), F32), pltpu.VMEM((grows, RET_V), F32),
                        pltpu.VMEM((STATE_SLOTS, SAMPLE_GROUP, hd, RET_DK, RET_DV), F32),
                        pltpu.SemaphoreType.DMA((STATE_SLOTS,))],
        input_output_aliases=aliases,
        compiler_params=pltpu.CompilerParams(dimension_semantics=("arbitrary",) * 3,
                                             vmem_limit_bytes=VMEM_LIMIT_HIGH),
        name="ret_mixer")(*args)


def _gated_conv(u, zb, cw_ref, prev1, prev2, t):
    u1 = jnp.where(t >= 1, pltpu.roll(u, 1, 0), prev1)
    u2 = jnp.where(t >= 2, pltpu.roll(u, 2, 0), jnp.where(t == 1, prev1, prev2))
    conv = cw_ref[0:1, :] * u2 + cw_ref[1:2, :] * u1 + cw_ref[2:3, :] * u
    return zb * conv


def _conv_kernel(*refs, n_cast, n_sample, sample_seq, n_prompt, tiles_per_seq):
    (xm_ref, xs_ref, wb_ref, wc_ref, wv_ref, wz_ref, cw_ref, st_ref), refs = refs[:8], refs[8:]
    cast_in, refs = refs[:n_cast], refs[n_cast:]
    (om_ref, os_ref, stp_ref, sts_ref), refs = refs[:4], refs[4:]
    cast_out, (wbf_ref, meta_tail_ref, tail_ref) = refs[:n_cast], refs[n_cast:]
    t_step = pl.program_id(1)
    tn = om_ref.shape[1]
    n_sample_rows = n_sample * sample_seq

    for n, (src_ref, dst_ref) in enumerate(zip(cast_in, cast_out)):
        @pl.when(t_step <= n)
        def _(src_ref=src_ref, dst_ref=dst_ref):
            dst_ref[...] = src_ref[...].astype(BF16)

    def project(x):
        u = _dot(x, wbf_ref[1]) * _dot(x, wbf_ref[2])
        zb = _silu(_dot(x, wbf_ref[3])) * _dot(x, wbf_ref[0])
        return u, zb

    @pl.when(t_step == 0)
    def _():
        wbf_ref[0] = wb_ref[...].astype(BF16)
        wbf_ref[1] = wc_ref[...].astype(BF16)
        wbf_ref[2] = wv_ref[...].astype(BF16)
        wbf_ref[3] = wz_ref[...].astype(BF16)
        u, zb = project(xs_ref[...])
        us = u[0:n_sample_rows]
        st = st_ref[...]
        shape3 = (n_sample, sample_seq, tn)
        prev2 = jnp.broadcast_to(st[:, 0:1, :], shape3).reshape(n_sample_rows, tn)
        prev1 = jnp.broadcast_to(st[:, 1:2, :], shape3).reshape(n_sample_rows, tn)
        t = lax.broadcasted_iota(jnp.int32, us.shape, 0) % sample_seq
        os_ref[0:n_sample_rows, :] = _gated_conv(us, zb[0:n_sample_rows], cw_ref, prev1, prev2,
                                                 t).astype(os_ref.dtype)
        sts_ref[...] = us.reshape(shape3)[:, sample_seq - (CONV_WIDTH - 1):sample_seq, :]
        um = u[n_sample_rows:]
        t = lax.broadcasted_iota(jnp.int32, um.shape, 0) % N_META
        zero = jnp.zeros((), F32)
        os_ref[n_sample_rows:, :] = _gated_conv(um, zb[n_sample_rows:], cw_ref, zero, zero,
                                                t).astype(os_ref.dtype)
        for bb in range(n_prompt):
            meta_tail_ref[bb] = um[(bb + 1) * N_META - SUBLANES:(bb + 1) * N_META]

    @pl.when(t_step > 0)
    def _():
        tile = t_step - 1
        bb = tile // tiles_per_seq
        first = tile % tiles_per_seq == 0
        u, zb = project(xm_ref[...])
        rows = u.shape[0]
        tail = jnp.where(first, meta_tail_ref[bb], tail_ref[...])
        prev1 = tail[SUBLANES - 1:SUBLANES, :]
        prev2 = tail[SUBLANES - 2:SUBLANES - 1, :]
        t = lax.broadcasted_iota(jnp.int32, u.shape, 0)
        om_ref[...] = _gated_conv(u, zb, cw_ref, prev1, prev2, t).astype(om_ref.dtype)
        tail_ref[...] = u[rows - SUBLANES:rows]

        @pl.when(tile % tiles_per_seq == tiles_per_seq - 1)
        def _():
            stp_ref[0] = u[rows - (CONV_WIDTH - 1):rows]


def _conv_mixer(xm, xs, w_all, cw_all, layer, state_all, cast_jobs, n_prompt, seq, n_sample, sample_seq):
    rm, k = xm.shape
    rs = xs.shape[0]
    e = CONV_E
    tn, tm = CONV_TN, MAIN_TILE
    nt = e // tn
    n_main = rm // tm
    tiles_per_seq = seq // tm
    ns = CONV_WIDTH - 1

    def main_tile(t):
        return jnp.maximum(t - 1, 0)

    w_specs = [pl.BlockSpec((None, k, tn), functools.partial(lambda j, t, q: (layer, 0, q * nt + j), q=q))
               for q in range(4)]
    cast_in_specs, cast_out_specs, cast_shapes, cast_args = [], [], [], []
    for n, (w_src, src_layer) in enumerate(cast_jobs):
        _, wk, wn = w_src.shape
        rows = wk // (nt * (n + 1))

        def blk(j, t, n=n):
            return j * (n + 1) + jnp.minimum(t, n)

        cast_in_specs.append(pl.BlockSpec((None, rows, wn),
                                          functools.partial(lambda j, t, f, l: (l, f(j, t), 0), f=blk, l=src_layer)))
        cast_out_specs.append(pl.BlockSpec((rows, wn), functools.partial(lambda j, t, f: (f(j, t), 0), f=blk)))
        cast_shapes.append(jax.ShapeDtypeStruct((wk, wn), BF16))
        cast_args.append(w_src)
    kern = functools.partial(_conv_kernel, n_cast=len(cast_jobs), n_sample=n_sample, sample_seq=sample_seq,
                             n_prompt=n_prompt, tiles_per_seq=tiles_per_seq)
    return pl.pallas_call(
        kern, grid=(nt, n_main + 1),
        in_specs=[pl.BlockSpec((tm, k), lambda j, t: (main_tile(t), 0)),
                  pl.BlockSpec((rs, k), lambda j, t: (0, 0))] + w_specs + [
                  pl.BlockSpec((None, CONV_WIDTH, tn), lambda j, t: (layer, 0, j)),
                  pl.BlockSpec((None, n_sample, ns, tn), lambda j, t: (layer, 0, 0, j))] + cast_in_specs,
        out_specs=[pl.BlockSpec((tm, tn), lambda j, t: (main_tile(t), j)),
                   pl.BlockSpec((rs, tn), lambda j, t: (0, j)),
                   pl.BlockSpec((1, ns, tn), lambda j, t: (main_tile(t) // tiles_per_seq, 0, j)),
                   pl.BlockSpec((n_sample, ns, tn), lambda j, t: (0, 0, j))] + cast_out_specs,
        out_shape=[jax.ShapeDtypeStruct((rm, e), BF16),
                   jax.ShapeDtypeStruct((rs, e), BF16),
                   jax.ShapeDtypeStruct((n_prompt, ns, e), F32),
                   jax.ShapeDtypeStruct((n_sample, ns, e), F32)] + cast_shapes,
        scratch_shapes=[pltpu.VMEM((4, k, tn), BF16),
                        pltpu.VMEM((n_prompt, SUBLANES, tn), F32),
                        pltpu.VMEM((SUBLANES, tn), F32)],
        compiler_params=_params("arbitrary", "arbitrary"),
        name="conv_mixer")(xm, xs, w_all, w_all, w_all, w_all, cw_all, state_all, *cast_args)


def _rope_tables(pos):
    freqs = ROPE_BASE ** (-jnp.arange(HALF, dtype=F32) * 2.0 / RET_DK)
    ang = pos.astype(F32)[:, None] * freqs[None, :]
    return jnp.cos(ang), jnp.sin(ang)


def _decay_tables(length):
    lg = jnp.log1p(-jnp.exp2(-5.0 - jnp.arange(RET_HEADS, dtype=F32)))
    idx = jnp.arange(length, dtype=F32)
    diff = idx[:, None] - idx[None, :]
    dmask = jnp.exp(jnp.where(diff[None] >= 0, diff[None] * lg[:, None, None], -jnp.inf))
    q_decay = jnp.exp((idx + 1.0)[None, :] * lg[:, None])[:, :, None]
    k_decay = jnp.exp((length - 1.0 - idx)[None, :] * lg[:, None])[:, :, None]
    s_decay = jnp.exp(length * lg)
    return dmask, q_decay, k_decay, s_decay


def kernel(x_prompt, x_sample, state_ret, state_conv, meta, w_ret_in, w_ret_out, w_conv_in, conv_w,
           w_conv_out, ln_g, ln_b):
    bp, lp, d = x_prompt.shape
    bs, ls, _ = x_sample.shape
    n_sample_rows = bs * ls

    xm = x_prompt.reshape(bp * lp, d)
    xs = jnp.concatenate([x_sample.reshape(n_sample_rows, d),
                          jnp.tile(meta.astype(x_prompt.dtype), (bp, 1))], axis=0)
    xm_bf, xs_bf = xm, xs.astype(BF16)

    rope_m = _rope_tables(N_META + jnp.arange(lp, dtype=jnp.int32))
    pos_s = jnp.concatenate([jnp.tile(PAST_LEN + jnp.arange(ls, dtype=jnp.int32), bs),
                             jnp.tile(jnp.arange(N_META, dtype=jnp.int32), bp)])
    rope_s = _rope_tables(pos_s)

    dm_p, qd_p, kd_p, sdec_p = _decay_tables(PROMPT_CHUNK)
    _, _, kd_m, _ = _decay_tables(N_META)
    dm_s, qd_s, kd_s, sdec_s = _decay_tables(ls)
    tabs = dict(dm_p=dm_p, qd_p=qd_p, kd_p=kd_p, sdec_p=sdec_p, kd_m=kd_m,
                dm_s=dm_s, qd_s=qd_s, kd_s=kd_s, sdec_s=sdec_s)

    w_ret_out_next = w_ret_out[0].astype(BF16)

    ret_p = ret_s = None
    conv_p, conv_s = [], []
    for i in range(DEPTH):
        j = i // 2
        g_i, b_i = ln_g[i][None, :], ln_b[i][None, :]
        if i % 2 == 0:
            pm, ps = _ret_proj(xm_bf, xs_bf, w_ret_in, j, rope_m, rope_s, lp)
            am, a_meta, a_sample, ret_p, ret_s = _ret_mixer(pm, ps, state_ret, j, ret_p, ret_s, tabs,
                                                            bp, lp, bs, ls)
            a_small = jnp.concatenate([a_sample, a_meta], axis=0)
            w_out = w_ret_out_next
        else:
            jobs = [(w_conv_out, j)]
            if j + 1 < w_ret_out.shape[0]:
                jobs.append((w_ret_out, j + 1))
            am, a_small, sp, ss, w_out, *nxt = _conv_mixer(xm_bf, xs_bf, w_conv_in, conv_w, j, state_conv,
                                                           jobs, bp, lp, bs, ls)
            if nxt:
                w_ret_out_next = nxt[0]
            conv_p.append(sp)
            conv_s.append(ss)
        xm, xm_bf = _out_ln(am, w_out, xm, g_i, b_i, OUT_TM, OUT_SUB)
        if i + 1 < DEPTH:
            half = xs.shape[0] // 2
            xs, xs_bf = _out_ln(a_small, w_out, xs, g_i, b_i, half, (half // 2, half // 2))
        else:
            xs, _ = _out_ln(a_small, w_out, xs, g_i, b_i, OUT_TM, OUT_SUB, rows=n_sample_rows)

    y_prompt = xm.reshape(bp, lp, d)
    y_sample = xs.reshape(bs, ls, d)
    return (y_prompt, y_sample, ret_p, ret_s, jnp.stack(conv_p), jnp.stack(conv_s))
```
